```python
import jax, jax.numpy as jnp
from jax import lax
import numpy as np

D_MODEL = 1024
BATCH = 16
SEQ = 4096
DEPTH = 4

D_SGU = 1024
SGU_CHUNK = 128
SGU_GROUPS = 8
SGU_GROUP_DIM = D_SGU // SGU_GROUPS
D_RNN = 1024
RG_HEADS = 8
RG_HEAD_DIM = D_RNN // RG_HEADS
CONV_WIDTH = 4
RG_C = 8.0
IN_WIDTH = 2 * D_SGU + 2 * D_RNN + 2 * D_MODEL
IN_SPLITS = (D_SGU, 2 * D_SGU, 2 * D_SGU + D_RNN, 2 * D_SGU + 2 * D_RNN, 2 * D_SGU + 2 * D_RNN + D_MODEL)
N_EXPERTS = 64
TOP_K = 8
N_GROUPS = 8
TOPK_GROUPS = 4
D_EXPERT = 256
D_SHARED = 256
ROUTED_SCALE = 2.5
MOE_BLOCK = 128
DEEPNORM_ALPHA = (2 * DEPTH) ** 0.25
DEEPNORM_BETA = (8 * DEPTH) ** -0.25
LN_EPS = 1e-5

kernel_name = 'hybrid_sgu_rglru_moe_deepnorm_adaln'


def _layer_norm(x, g, b):
    xf = x.astype(jnp.float32)
    mu = jnp.mean(xf, axis=-1, keepdims=True)
    var = jnp.mean(jnp.square(xf - mu), axis=-1, keepdims=True)
    return ((xf - mu) * lax.rsqrt(var + LN_EPS)).astype(x.dtype) * g + b


def _chunk_spatial(v, w_s, b_s):
    B, S, C = v.shape
    vc = v.reshape(B, S // SGU_CHUNK, SGU_CHUNK, SGU_GROUPS, SGU_GROUP_DIM)
    w = w_s * jnp.tril(jnp.ones((SGU_CHUNK, SGU_CHUNK), w_s.dtype))
    out = jnp.einsum('gts,bnsgc->bntgc', w, vc) + b_s.T[:, :, None]
    return out.reshape(B, S, C)


def _causal_conv(x, w, b):
    S = x.shape[1]
    xp = jnp.pad(x, ((0, 0), (CONV_WIDTH - 1, 0), (0, 0)))
    return sum(xp[:, k:k + S] * w[k] for k in range(CONV_WIDTH)) + b


def _rg_lru(xb, wa, ba, wx, bx, lam):
    B, S, _ = xb.shape
    xh = xb.reshape(B, S, RG_HEADS, RG_HEAD_DIM)
    r = jax.nn.sigmoid(jnp.einsum('bshi,hij->bshj', xh, wa).reshape(B, S, D_RNN) + ba)
    i = jax.nn.sigmoid(jnp.einsum('bshi,hij->bshj', xh, wx).reshape(B, S, D_RNN) + bx)
    log_a = -RG_C * r.astype(jnp.float32) * jax.nn.softplus(-lam.astype(jnp.float32))
    a = jnp.exp(log_a)
    inp = jnp.sqrt(-jnp.expm1(2.0 * log_a)) * (i * xb).astype(jnp.float32)

    def combine(left, right):
        a1, b1 = left
        a2, b2 = right
        return a1 * a2, a2 * b1 + b2

    _, h = lax.associative_scan(combine, (a, inp), axis=1)
    return h.astype(xb.dtype)


def _route(h, w_r, b_r):
    B, S, _ = h.shape
    scores = jax.nn.sigmoid(jnp.einsum('bsd,de->bse', h.astype(jnp.float32), w_r.astype(jnp.float32)))
    biased = scores + b_r.astype(jnp.float32)
    grouped = biased.reshape(B, S, N_GROUPS, N_EXPERTS // N_GROUPS)
    group_score = lax.top_k(grouped, 2)[0].sum(-1)
    _, top_groups = lax.top_k(group_score, TOPK_GROUPS)
    group_keep = jax.nn.one_hot(top_groups, N_GROUPS, dtype=jnp.float32).sum(-2) > 0
    expert_keep = jnp.repeat(group_keep, N_EXPERTS // N_GROUPS, axis=-1)
    _, idx = lax.top_k(jnp.where(expert_keep, biased, -jnp.inf), TOP_K)
    w = jnp.take_along_axis(scores, idx, axis=-1)
    w = w / jnp.sum(w, axis=-1, keepdims=True) * ROUTED_SCALE
    return idx, w.astype(h.dtype)


def _moe_row(xs, idx, wts, w1, w3, w2):
    T, D = xs.shape
    K = idx.shape[1]
    E = w1.shape[0]
    n_blocks = -(-(T * K) // MOE_BLOCK) + E
    flat_e = idx.reshape(-1)
    order = jnp.argsort(flat_e)
    e_sorted = flat_e[order]
    tok_sorted = order // K
    counts = jnp.bincount(flat_e, length=E)
    padded = (counts + MOE_BLOCK - 1) // MOE_BLOCK * MOE_BLOCK
    pad_end = jnp.cumsum(padded)
    pad_start = pad_end - padded
    grp_start = jnp.cumsum(counts) - counts
    pos = jnp.arange(T * K) - grp_start[e_sorted] + pad_start[e_sorted]
    buf = jnp.zeros((n_blocks * MOE_BLOCK, D), xs.dtype).at[pos].set(xs[tok_sorted])
    block_e = jnp.minimum(jnp.searchsorted(pad_end, jnp.arange(n_blocks) * MOE_BLOCK, side='right'), E - 1)

    def expert_block(args):
        xb, e = args
        hid = jax.nn.silu(xb @ w1[e]) * (xb @ w3[e])
        return hid @ w2[e]

    yb = lax.map(expert_block, (buf.reshape(n_blocks, MOE_BLOCK, D), block_e))
    y = yb.reshape(-1, D)[pos] * wts.reshape(-1)[order][:, None]
    return jax.ops.segment_sum(y, tok_sorted, num_segments=T)


def setup_inputs(seed: int = 0) -> dict:
    key = jax.random.key(seed)
    ks = iter(jax.random.split(key, 40))
    L, D = DEPTH, D_MODEL
    beta = DEEPNORM_BETA

    def nrm(shape, scale):
        return jax.random.normal(next(ks), shape, jnp.float32) * scale

    a0 = jax.random.uniform(next(ks), (L, D_RNN), jnp.float32, minval=0.9, maxval=0.999)
    rg_lambda = jnp.log(a0) - jnp.log1p(-a0)
    return {
        'x': nrm((BATCH, SEQ, D), 1.0),
        'c': nrm((BATCH, D), 1.0),
        'ada_w': nrm((L, D, 6 * D), 0.5 * D ** -0.5),
        'ada_b': nrm((L, 6 * D), 0.02),
        'w_in': nrm((L, D, IN_WIDTH), D ** -0.5),
        'sgu_ln_g': 1.0 + nrm((L, D_SGU), 0.02),
        'sgu_ln_b': nrm((L, D_SGU), 0.02),
        'sgu_w': nrm((L, SGU_GROUPS, SGU_CHUNK, SGU_CHUNK), SGU_CHUNK ** -0.5),
        'sgu_b': 1.0 + nrm((L, SGU_GROUPS, SGU_CHUNK), 0.02),
        'conv_w': nrm((L, CONV_WIDTH, D_RNN), CONV_WIDTH ** -0.5),
        'conv_b': nrm((L, D_RNN), 0.02),
        'rg_wa': nrm((L, RG_HEADS, RG_HEAD_DIM, RG_HEAD_DIM), RG_HEAD_DIM ** -0.5),
        'rg_ba': nrm((L, D_RNN), 0.02),
        'rg_wx': nrm((L, RG_HEADS, RG_HEAD_DIM, RG_HEAD_DIM), RG_HEAD_DIM ** -0.5),
        'rg_bx': nrm((L, D_RNN), 0.02),
        'rg_lambda': rg_lambda,
        'w_branch_a': nrm((L, D_SGU, D), beta * D_SGU ** -0.5),
        'w_branch_b': nrm((L, D_RNN, D), beta * D_RNN ** -0.5),
        'w_out': nrm((L, D, D), beta * D ** -0.5),
        'ln1_g': 1.0 + nrm((L, D), 0.02),
        'ln1_b': nrm((L, D), 0.02),
        'router_w': nrm((L, D, N_EXPERTS), D ** -0.5),
        'router_b': nrm((L, N_EXPERTS), 0.01),
        'exp_w1': nrm((L, N_EXPERTS, D, D_EXPERT), D ** -0.5),
        'exp_w3': nrm((L, N_EXPERTS, D, D_EXPERT), D ** -0.5),
        'exp_w2': nrm((L, N_EXPERTS, D_EXPERT, D), beta * D_EXPERT ** -0.5),
        'sh_w1': nrm((L, D, D_SHARED), D ** -0.5),
        'sh_w3': nrm((L, D, D_SHARED), D ** -0.5),
        'sh_w2': nrm((L, D_SHARED, D), beta * D_SHARED ** -0.5),
        'ln2_g': 1.0 + nrm((L, D), 0.02),
        'ln2_b': nrm((L, D), 0.02),
    }


def reference(x, c, ada_w, ada_b, w_in, sgu_ln_g, sgu_ln_b, sgu_w, sgu_b, conv_w, conv_b,
              rg_wa, rg_ba, rg_wx, rg_bx, rg_lambda, w_branch_a, w_branch_b, w_out, ln1_g, ln1_b,
              router_w, router_b, exp_w1, exp_w3, exp_w2, sh_w1, sh_w3, sh_w2, ln2_g, ln2_b):
    c_act = jax.nn.silu(c)
    for l in range(DEPTH):
        ada = c_act @ ada_w[l] + ada_b[l]
        shift1, scale1, gate1, shift2, scale2, gate2 = [t[:, None, :] for t in jnp.split(ada, 6, axis=-1)]

        h = x * (1.0 + scale1) + shift1
        z = h @ w_in[l]
        u, v, rnn_gate, rnn_in, gate_a, gate_b = jnp.split(z, IN_SPLITS, axis=-1)
        v = _layer_norm(jax.nn.gelu(v), sgu_ln_g[l], sgu_ln_b[l])
        y_a = jax.nn.gelu(u) * _chunk_spatial(v, sgu_w[l], sgu_b[l])
        r_in = _causal_conv(rnn_in, conv_w[l], conv_b[l])
        y_b = jax.nn.gelu(rnn_gate) * _rg_lru(r_in, rg_wa[l], rg_ba[l], rg_wx[l], rg_bx[l], rg_lambda[l])
        merged = jax.nn.sigmoid(gate_a) * (y_a @ w_branch_a[l]) + jax.nn.sigmoid(gate_b) * (y_b @ w_branch_b[l])
        mix = merged @ w_out[l]
        x = _layer_norm(DEEPNORM_ALPHA * x + gate1 * mix, ln1_g[l], ln1_b[l])

        h2 = x * (1.0 + scale2) + shift2
        idx, wts = _route(h2, router_w[l], router_b[l])
        w1, w3, w2 = exp_w1[l], exp_w3[l], exp_w2[l]
        routed = lax.map(lambda a: _moe_row(a[0], a[1], a[2], w1, w3, w2), (h2, idx, wts))
        shared = (jax.nn.silu(h2 @ sh_w1[l]) * (h2 @ sh_w3[l])) @ sh_w2[l]
        x = _layer_norm(DEEPNORM_ALPHA * x + gate2 * (routed + shared), ln2_g[l], ln2_b[l])
    return x
```

```python
import functools

import jax
import jax.numpy as jnp
from jax import lax
from jax.experimental import pallas as pl
from jax.experimental.pallas import tpu as pltpu

SGU_CHUNK = 128
SGU_GROUPS = 8
RG_HEADS = 8
CONV_WIDTH = 4
RG_C = 8.0
N_EXPERTS = 64
TOP_K = 8
N_GROUPS = 8
GROUP_SIZE = N_EXPERTS // N_GROUPS
TOPK_GROUPS = 4
ROUTED_SCALE = 2.5
LN_EPS = 1e-5

SUBLANES = 8
TOKEN_TILE = 256
SLOT_ROWS = TOP_K * TOKEN_TILE + N_EXPERTS * SUBLANES
EXPERT_BLOCK = 512
VMEM_LIMIT = 56 * 1024 * 1024

_F32 = jnp.float32
_BF16 = jnp.bfloat16


def _gelu(x):
    return jax.nn.gelu(x, approximate=True)


def _layer_norm(x, g, b):
    mu = jnp.mean(x, axis=-1, keepdims=True)
    xc = x - mu
    var = jnp.mean(xc * xc, axis=-1, keepdims=True)
    return xc * lax.rsqrt(var + LN_EPS) * g + b


def _pack_halves(v):
    n = v.shape[1] // 2
    lo = lax.bitcast_convert_type(v[:, :n], jnp.uint32)
    hi = lax.bitcast_convert_type(v[:, n:], jnp.uint32)
    return (hi & jnp.uint32(0xFFFF0000)) | (lo >> 16)


def _unpack_halves(u):
    lo = lax.bitcast_convert_type(u << 16, _F32).astype(_BF16)
    hi = lax.bitcast_convert_type(u & jnp.uint32(0xFFFF0000), _F32).astype(_BF16)
    return lo, hi


def _ada_kernel(c_ref, w_ref, b_ref, o_ref):
    c = c_ref[...]
    c_act = c * jax.nn.sigmoid(c)
    o_ref[0] = jnp.dot(c_act, w_ref[0], preferred_element_type=_F32,
                       precision=lax.Precision.HIGHEST) + b_ref[0]


def _ada_all(c, ada_w, ada_b):
    L, D, W = ada_w.shape
    B = c.shape[0]
    nb = W // D
    return pl.pallas_call(
        _ada_kernel,
        grid=(L, nb),
        in_specs=[pl.BlockSpec((B, D), lambda l, n: (0, 0)),
                  pl.BlockSpec((1, D, D), lambda l, n: (l, 0, n)),
                  pl.BlockSpec((1, 1, D), lambda l, n: (l, 0, n))],
        out_specs=pl.BlockSpec((1, B, D), lambda l, n: (l, 0, n)),
        out_shape=jax.ShapeDtypeStruct((L, B, W), _F32),
        name="ada",
    )(c, ada_w, ada_b.reshape(L, 1, W))


def _scan_rows(a, b):
    tm, C = a.shape
    row = lax.broadcasted_iota(jnp.int32, a.shape, 0)
    d = 1
    while d < tm:
        if d < SUBLANES:
            keep = row >= d
            a_sh = jnp.where(keep, pltpu.roll(a, d, axis=0), 1.0)
            b_sh = jnp.where(keep, pltpu.roll(b, d, axis=0), 0.0)
        else:
            a_sh = jnp.concatenate([jnp.ones((d, C), _F32), a[:tm - d]], axis=0)
            b_sh = jnp.concatenate([jnp.zeros((d, C), _F32), b[:tm - d]], axis=0)
        b = b + a * b_sh
        a = a * a_sh
        d *= 2
    return a, b


def _mixer_kernel(alpha, x_ref, shift_ref, scale_ref, gate_ref, win_ref, slg_ref, slb_ref, sw_ref, sbt_ref,
                  cw_ref, cb_ref, wa_ref, ba_ref, wx_ref, bx_ref, lam_ref, wba_ref, wbb_ref, wo_ref,
                  lg_ref, lb_ref, o_ref, rbuf, hprev, sp_ref):
    tm, D = x_ref.shape[1], x_ref.shape[2]
    C = SGU_CHUNK
    si = pl.program_id(1)

    @pl.when(si == 0)
    def _():
        rbuf[0:SUBLANES, :] = jnp.zeros((SUBLANES, D), _F32)
        hprev[...] = jnp.zeros_like(hprev)

    x = x_ref[0]
    h = (x * (1.0 + scale_ref[0]) + shift_ref[0]).astype(_BF16)

    def proj(k):
        return jnp.dot(h, win_ref[:, k * D:(k + 1) * D], preferred_element_type=_F32)

    vn = _layer_norm(_gelu(proj(1)), slg_ref[...], slb_ref[...]).astype(_BF16)
    tri = (lax.broadcasted_iota(jnp.int32, (C, C), 0) >= lax.broadcasted_iota(jnp.int32, (C, C), 1))
    for g in range(SGU_GROUPS):
        wm = jnp.where(tri, sw_ref[g], 0.0).astype(_BF16)
        bias = sbt_ref[:, g:g + 1]
        for cc in range(tm // C):
            blk = jnp.dot(wm, vn[cc * C:(cc + 1) * C, g * C:(g + 1) * C], preferred_element_type=_F32)
            sp_ref[cc * C:(cc + 1) * C, g * C:(g + 1) * C] = blk + bias
    y_a = (_gelu(proj(0)) * sp_ref[...]).astype(_BF16)

    rbuf[SUBLANES:SUBLANES + tm, :] = proj(3)
    r_in = cb_ref[...] + cw_ref[CONV_WIDTH - 1:CONV_WIDTH, :] * rbuf[SUBLANES:SUBLANES + tm, :]
    for dlt in range(1, CONV_WIDTH):
        r_in = r_in + cw_ref[CONV_WIDTH - 1 - dlt:CONV_WIDTH - dlt, :] * rbuf[SUBLANES - dlt:SUBLANES - dlt + tm, :]
    rbuf[0:SUBLANES, :] = rbuf[tm:tm + SUBLANES, :]
    r_bf = r_in.astype(_BF16)
    hd = D // RG_HEADS
    ra = jnp.concatenate([jnp.dot(r_bf[:, k * hd:(k + 1) * hd], wa_ref[k], preferred_element_type=_F32)
                          for k in range(RG_HEADS)], axis=1)
    rx = jnp.concatenate([jnp.dot(r_bf[:, k * hd:(k + 1) * hd], wx_ref[k], preferred_element_type=_F32)
                          for k in range(RG_HEADS)], axis=1)
    r_gate = jax.nn.sigmoid(ra + ba_ref[...])
    i_gate = jax.nn.sigmoid(rx + bx_ref[...])
    neg_lam = -lam_ref[...]
    softplus = jnp.maximum(neg_lam, 0.0) + jnp.log1p(jnp.exp(-jnp.abs(neg_lam)))
    log_a = (-RG_C) * r_gate * softplus
    a = jnp.exp(log_a)
    t = jnp.tanh(log_a)
    inp = jnp.sqrt(-2.0 * t / (1.0 - t)) * (i_gate * r_in)
    a_cum, h_loc = _scan_rows(a, inp)
    h_seq = h_loc + a_cum * hprev[...]
    hprev[...] = h_seq[tm - 1:tm, :]
    y_b = (_gelu(proj(2)) * h_seq).astype(_BF16)

    merged = (jax.nn.sigmoid(proj(4)) * jnp.dot(y_a, wba_ref[...], preferred_element_type=_F32)
              + jax.nn.sigmoid(proj(5)) * jnp.dot(y_b, wbb_ref[...], preferred_element_type=_F32))
    mix = jnp.dot(merged.astype(_BF16), wo_ref[...], preferred_element_type=_F32)
    o_ref[0] = _layer_norm(alpha * x + gate_ref[0] * mix, lg_ref[...], lb_ref[...])


def _const_spec(shape):
    nd = len(shape)
    return pl.BlockSpec(shape, lambda b, s: (0,) * nd, pipeline_mode=pl.Buffered(1))


def _mixer(alpha, x, shift, scale, gate, w_in, slg, slb, sw, sbt, cw, cb, wa, ba, wx, bx, lam, wba, wbb, wo, lg, lb):
    B, S, D = x.shape
    tm = TOKEN_TILE
    vec = pl.BlockSpec((1, 1, D), lambda b, s: (b, 0, 0))
    row = lambda a: a.reshape(1, D)
    consts = [w_in, row(slg), row(slb), sw, sbt, cw, row(cb), wa, row(ba), wx, row(bx), row(lam), wba, wbb, wo,
              row(lg), row(lb)]
    return pl.pallas_call(
        functools.partial(_mixer_kernel, alpha),
        grid=(B, S // tm),
        in_specs=[pl.BlockSpec((1, tm, D), lambda b, s: (b, s, 0)), vec, vec, vec]
                 + [_const_spec(a.shape) for a in consts],
        out_specs=pl.BlockSpec((1, tm, D), lambda b, s: (b, s, 0)),
        out_shape=jax.ShapeDtypeStruct((B, S, D), _F32),
        scratch_shapes=[pltpu.VMEM((tm + SUBLANES, D), _F32), pltpu.VMEM((1, D), _F32), pltpu.VMEM((tm, D), _F32)],
        compiler_params=pltpu.CompilerParams(dimension_semantics=("arbitrary", "arbitrary"),
                                             vmem_limit_bytes=VMEM_LIMIT),
        name="mixer",
    )(x, shift, scale, gate, *consts)


def _expert_of_row(p):
    return (p % N_GROUPS) * GROUP_SIZE + p // N_GROUPS


def _router_kernel(x_ref, shift_ref, scale_ref, wr_ref, br_ref, slot_ref, wk_ref, cnt_ref):
    tm = x_ref.shape[0]
    G, J = N_GROUPS, GROUP_SIZE
    h2 = x_ref[...] * (1.0 + scale_ref[0]) + shift_ref[0]
    logits = lax.dot_general(wr_ref[...], h2, (((1,), (1,)), ((), ())), preferred_element_type=_F32,
                             precision=lax.Precision.HIGHEST)
    scores = jax.nn.sigmoid(logits)
    biased = scores + br_ref[...]
    slabs = [biased[j * G:(j + 1) * G, :] for j in range(J)]

    m1, m2 = slabs[0], jnp.full_like(slabs[0], -jnp.inf)
    for v in slabs[1:]:
        m2 = jnp.maximum(m2, jnp.minimum(m1, v))
        m1 = jnp.maximum(m1, v)
    gs = m1 + m2
    gidx = lax.broadcasted_iota(jnp.int32, gs.shape, 0)
    grank = jnp.zeros(gs.shape, jnp.int32)
    for g2 in range(G):
        other = gs[g2:g2 + 1, :]
        beats = (other > gs) | ((other == gs) & (g2 < gidx))
        grank = grank + beats.astype(jnp.int32)
    keep = grank < TOPK_GROUPS
    masked = [jnp.where(keep, v, -jnp.inf) for v in slabs]

    ranks = [jnp.zeros(gs.shape, jnp.int32) for _ in range(J)]
    for j2 in range(J):
        for g2 in range(G):
            other = masked[j2][g2:g2 + 1, :]
            for j in range(J):
                tie = (gidx > g2) if j2 >= j else (gidx >= g2)
                beats = (other > masked[j]) | ((other == masked[j]) & tie)
                ranks[j] = ranks[j] + beats.astype(jnp.int32)
    sel = jnp.concatenate([(ranks[j] < TOP_K) & keep for j in range(J)], axis=0)
    sel_f = sel.astype(_F32)
    w_sel = scores * sel_f
    wn = w_sel / jnp.sum(w_sel, axis=0, keepdims=True) * ROUTED_SCALE

    sel_b = sel_f.astype(_BF16)
    E = N_EXPERTS
    upper = (lax.broadcasted_iota(jnp.int32, (tm, tm), 0) < lax.broadcasted_iota(jnp.int32, (tm, tm), 1))
    tok_rank = jnp.dot(sel_b, upper.astype(_BF16), preferred_element_type=_F32)
    cnt = jnp.sum(sel_f, axis=1, keepdims=True)
    cnt_tiles = jnp.floor((cnt + (SUBLANES - 1)) * (1.0 / SUBLANES))
    lower = (lax.broadcasted_iota(jnp.int32, (E, E), 1) < lax.broadcasted_iota(jnp.int32, (E, E), 0))
    lower_b = lower.astype(_BF16)
    start = jnp.dot(lower_b, jnp.broadcast_to(cnt_tiles, (E, 128)).astype(_BF16),
                    preferred_element_type=_F32)[:, 0:1] * float(SUBLANES)
    slot = start + tok_rank
    k_of = jnp.dot(lower_b, sel_b, preferred_element_type=_F32)
    slot_rows, w_rows = [], []
    for k in range(TOP_K):
        pick = sel & (k_of == float(k))
        slot_rows.append(jnp.sum(jnp.where(pick, slot, 0.0), axis=0, keepdims=True))
        w_rows.append(jnp.sum(jnp.where(pick, wn, 0.0), axis=0, keepdims=True))
    slot_ref[0] = jnp.concatenate(slot_rows, axis=0).astype(jnp.int32)
    wk_ref[0] = jnp.concatenate(w_rows, axis=0)
    ones = jnp.ones((SUBLANES, tm), _BF16)
    cnt_ref[0] = lax.dot_general(ones, sel_b, (((1,), (1,)), ((), ())), preferred_element_type=_F32)


def _router(x1, shift, scale, wr_rows, br_rows, tiles_per_seq):
    T, D = x1.shape
    tm = TOKEN_TILE
    nt = T // tm
    vec = pl.BlockSpec((1, 1, D), lambda j: (j // tiles_per_seq, 0, 0))
    return pl.pallas_call(
        _router_kernel,
        grid=(nt,),
        in_specs=[pl.BlockSpec((tm, D), lambda j: (j, 0)), vec, vec,
                  pl.BlockSpec((N_EXPERTS, D), lambda j: (0, 0)),
                  pl.BlockSpec((N_EXPERTS, 1), lambda j: (0, 0))],
        out_specs=[pl.BlockSpec((1, TOP_K, tm), lambda j: (j, 0, 0)),
                   pl.BlockSpec((1, TOP_K, tm), lambda j: (j, 0, 0)),
                   pl.BlockSpec((1, SUBLANES, N_EXPERTS), lambda j: (j, 0, 0))],
        out_shape=[jax.ShapeDtypeStruct((nt, TOP_K, tm), jnp.int32),
                   jax.ShapeDtypeStruct((nt, TOP_K, tm), _F32),
                   jax.ShapeDtypeStruct((nt, SUBLANES, N_EXPERTS), _F32)],
        compiler_params=pltpu.CompilerParams(dimension_semantics=("arbitrary",), vmem_limit_bytes=VMEM_LIMIT),
        name="router",
    )(x1, shift, scale, wr_rows, br_rows)


def _segment_copies(cnt_ref, lstart_ref, goff_ref, j, local_ref, global_ref, sem, to_global):
    for p in range(N_EXPERTS):
        n = pl.multiple_of(cnt_ref[j * N_EXPERTS + p], SUBLANES)
        ls = pl.multiple_of(lstart_ref[j * N_EXPERTS + p], SUBLANES)
        go = pl.multiple_of(goff_ref[j * N_EXPERTS + p], SUBLANES)

        @pl.when(n > 0)
        def _():
            loc = local_ref.at[pl.ds(ls, n)]
            glo = global_ref.at[pl.ds(go, n)]
            if to_global:
                pltpu.make_async_copy(loc, glo, sem).start()
            else:
                pltpu.make_async_copy(glo, loc, sem).start()


def _wait_rows(tot_ref, j, local_ref, global_ref, sem, to_global):
    n = pl.multiple_of(tot_ref[j], SUBLANES)

    @pl.when(n > 0)
    def _():
        loc = local_ref.at[pl.ds(0, n)]
        glo = global_ref.at[pl.ds(0, n)]
        if to_global:
            pltpu.make_async_copy(loc, glo, sem).wait()
        else:
            pltpu.make_async_copy(glo, loc, sem).wait()


def _dispatch_kernel(cnt_ref, lstart_ref, goff_ref, tot_ref, x_ref, shift_ref, scale_ref, slot_ref, xg_prev_ref,
                     xg_ref, xs_ref, sem):
    del xg_prev_ref
    j = pl.program_id(0)
    tm, D = x_ref.shape
    h2 = (x_ref[...] * (1.0 + scale_ref[0]) + shift_ref[0]).astype(_BF16)
    slots = slot_ref[0]
    s_iota = lax.broadcasted_iota(jnp.int32, (SLOT_ROWS, tm), 0)
    hit = s_iota == slots[0:1, :]
    for k in range(1, TOP_K):
        hit = hit | (s_iota == slots[k:k + 1, :])
    onehot = jnp.where(hit, 1.0, 0.0).astype(_BF16)
    half = D // 2
    lo = jnp.dot(onehot, h2[:, :half], preferred_element_type=_F32)
    hi = jnp.dot(onehot, h2[:, half:], preferred_element_type=_F32)
    xs_ref[...] = ((lax.bitcast_convert_type(hi, jnp.uint32) & jnp.uint32(0xFFFF0000))
                   | (lax.bitcast_convert_type(lo, jnp.uint32) >> 16))
    _segment_copies(cnt_ref, lstart_ref, goff_ref, j, xs_ref, xg_ref, sem, True)
    _wait_rows(tot_ref, j, xs_ref, xg_ref, sem, True)


def _dispatch(tables, x1, shift, scale, slots, xg_prev, tiles_per_seq):
    T, D = x1.shape
    tm = TOKEN_TILE
    nt = T // tm
    vec = pl.BlockSpec((1, 1, D), lambda j, *_: (j // tiles_per_seq, 0, 0))
    return pl.pallas_call(
        _dispatch_kernel,
        grid_spec=pltpu.PrefetchScalarGridSpec(
            num_scalar_prefetch=4,
            grid=(nt,),
            in_specs=[pl.BlockSpec((tm, D), lambda j, *_: (j, 0)), vec, vec,
                      pl.BlockSpec((1, TOP_K, tm), lambda j, *_: (j, 0, 0)),
                      pl.BlockSpec(memory_space=pl.ANY)],
            out_specs=pl.BlockSpec(memory_space=pl.ANY),
            scratch_shapes=[pltpu.VMEM((SLOT_ROWS, D // 2), jnp.uint32), pltpu.SemaphoreType.DMA],
        ),
        out_shape=jax.ShapeDtypeStruct(xg_prev.shape, jnp.uint32),
        input_output_aliases={8: 0},
        compiler_params=pltpu.CompilerParams(dimension_semantics=("arbitrary",), vmem_limit_bytes=VMEM_LIMIT),
        name="dispatch",
    )(*tables, x1, shift, scale, slots, xg_prev)


def _expert_kernel(blk_ref, nused_ref, x_ref, w1_ref, w3_ref, w2_ref, y_prev_ref, o_ref):
    del y_prev_ref
    i = pl.program_id(0)

    @pl.when(i < nused_ref[0])
    def _():
        half = w1_ref.shape[1] // 2
        lo, hi = _unpack_halves(x_ref[...])
        a = (jnp.dot(lo, w1_ref[0, :half, :], preferred_element_type=_F32)
             + jnp.dot(hi, w1_ref[0, half:, :], preferred_element_type=_F32))
        b = (jnp.dot(lo, w3_ref[0, :half, :], preferred_element_type=_F32)
             + jnp.dot(hi, w3_ref[0, half:, :], preferred_element_type=_F32))
        hid = (a * jax.nn.sigmoid(a) * b).astype(_BF16)
        y = jnp.dot(hid, w2_ref[0], preferred_element_type=_F32)
        o_ref[...] = _pack_halves(y.astype(_BF16).astype(_F32))


def _experts(blk_e, nused, xg, w1, w3, w2, yg_prev):
    R, Dh = xg.shape
    bm = EXPERT_BLOCK
    _, D, H = w1.shape
    rows = lambda i, blk, nu: (jnp.minimum(i, nu[0] - 1), 0)
    wsel = lambda i, blk, nu: (blk[i], 0, 0)
    return pl.pallas_call(
        _expert_kernel,
        grid_spec=pltpu.PrefetchScalarGridSpec(
            num_scalar_prefetch=2,
            grid=(R // bm,),
            in_specs=[pl.BlockSpec((bm, Dh), rows),
                      pl.BlockSpec((1, D, H), wsel), pl.BlockSpec((1, D, H), wsel), pl.BlockSpec((1, H, D), wsel),
                      pl.BlockSpec(memory_space=pl.ANY)],
            out_specs=pl.BlockSpec((bm, Dh), rows),
        ),
        out_shape=jax.ShapeDtypeStruct((R, Dh), jnp.uint32),
        input_output_aliases={6: 0},
        compiler_params=pltpu.CompilerParams(dimension_semantics=("arbitrary",), vmem_limit_bytes=VMEM_LIMIT),
        name="experts",
    )(blk_e, nused, xg, w1, w3, w2, yg_prev)


def _combine_kernel(alpha, cnt_ref, lstart_ref, goff_ref, tot_ref, x_ref, shift_ref, scale_ref, gate_ref,
                    slot_ref, wk_ref, yg_ref, s1_ref, s3_ref, s2_ref, lg_ref, lb_ref, o_ref, ys_ref, sem):
    j = pl.program_id(0)
    tm, D = x_ref.shape

    @pl.when(j == 0)
    def _():
        ys_ref[...] = jnp.zeros_like(ys_ref)

    _segment_copies(cnt_ref, lstart_ref, goff_ref, j, ys_ref, yg_ref, sem, False)

    x = x_ref[...]
    h2 = (x * (1.0 + scale_ref[0]) + shift_ref[0]).astype(_BF16)
    shared_hid = jnp.dot(h2, s1_ref[...], preferred_element_type=_F32)
    shared_hid = shared_hid * jax.nn.sigmoid(shared_hid) * jnp.dot(h2, s3_ref[...], preferred_element_type=_F32)
    shared = jnp.dot(shared_hid.astype(_BF16), s2_ref[...], preferred_element_type=_F32)

    slots = slot_ref[0]
    wk = wk_ref[0]
    s_iota = lax.broadcasted_iota(jnp.int32, (tm, SLOT_ROWS), 1)
    weights = jnp.zeros((tm, SLOT_ROWS), _F32)
    for k in range(TOP_K):
        weights = jnp.where(s_iota == slots[:, k:k + 1], wk[:, k:k + 1], weights)
    weights = weights.astype(_BF16)

    _wait_rows(tot_ref, j, ys_ref, yg_ref, sem, False)
    lo, hi = _unpack_halves(ys_ref[...])
    routed = jnp.concatenate([jnp.dot(weights, lo, preferred_element_type=_F32),
                              jnp.dot(weights, hi, preferred_element_type=_F32)], axis=1)
    o_ref[...] = _layer_norm(alpha * x + gate_ref[0] * (routed + shared), lg_ref[...], lb_ref[...])


def _combine(alpha, tables, x1, shift, scale, gate, slots_t, wk_t, yg, s1, s3, s2, lg, lb, tiles_per_seq):
    T, D = x1.shape
    tm = TOKEN_TILE
    nt = T // tm
    H = s1.shape[1]
    vec = pl.BlockSpec((1, 1, D), lambda j, *_: (j // tiles_per_seq, 0, 0))
    const = lambda shape: pl.BlockSpec(shape, lambda j, *_: (0,) * len(shape))
    return pl.pallas_call(
        functools.partial(_combine_kernel, alpha),
        grid_spec=pltpu.PrefetchScalarGridSpec(
            num_scalar_prefetch=4,
            grid=(nt,),
            in_specs=[pl.BlockSpec((tm, D), lambda j, *_: (j, 0)), vec, vec, vec,
                      pl.BlockSpec((1, tm, TOP_K), lambda j, *_: (j, 0, 0)),
                      pl.BlockSpec((1, tm, TOP_K), lambda j, *_: (j, 0, 0)),
                      pl.BlockSpec(memory_space=pl.ANY),
                      const((D, H)), const((D, H)), const((H, D)), const((1, D)), const((1, D))],
            out_specs=pl.BlockSpec((tm, D), lambda j, *_: (j, 0)),
            scratch_shapes=[pltpu.VMEM((SLOT_ROWS, D // 2), jnp.uint32), pltpu.SemaphoreType.DMA],
        ),
        out_shape=jax.ShapeDtypeStruct((T, D), _F32),
        compiler_params=pltpu.CompilerParams(dimension_semantics=("arbitrary",), vmem_limit_bytes=VMEM_LIMIT),
        name="combine",
    )(*tables, x1, shift, scale, gate, slots_t, wk_t, yg, s1, s3, s2, lg.reshape(1, D), lb.reshape(1, D))


def _round_up(v, m):
    return (v + m - 1) // m * m


def _dispatch_tables(cnt_f, n_blocks):
    cnt = _round_up(cnt_f.astype(jnp.int32), SUBLANES)
    lstart = jnp.cumsum(cnt, axis=1) - cnt
    per_row = jnp.sum(cnt, axis=0)
    region = _round_up(per_row, EXPERT_BLOCK)
    region_end = jnp.cumsum(region)
    goff = (region_end - region)[None, :] + jnp.cumsum(cnt, axis=0) - cnt
    tot = jnp.sum(cnt, axis=1)
    nused = (region_end[-1] // EXPERT_BLOCK).reshape(1)
    blk_row = jnp.searchsorted(region_end, jnp.arange(n_blocks, dtype=jnp.int32) * EXPERT_BLOCK, side="right")
    blk_row = jnp.minimum(blk_row, N_EXPERTS - 1).astype(jnp.int32)
    blk_e = _expert_of_row(blk_row)
    tables = (cnt.reshape(-1), lstart.reshape(-1).astype(jnp.int32), goff.reshape(-1).astype(jnp.int32),
              tot.astype(jnp.int32))
    return tables, blk_e.astype(jnp.int32), nused.astype(jnp.int32)


def kernel(x, c, ada_w, ada_b, w_in, sgu_ln_g, sgu_ln_b, sgu_w, sgu_b, conv_w, conv_b, rg_wa, rg_ba, rg_wx, rg_bx, rg_lambda, w_branch_a, w_branch_b, w_out, ln1_g, ln1_b, router_w, router_b, exp_w1, exp_w3, exp_w2, sh_w1, sh_w3, sh_w2, ln2_g, ln2_b):
    B, S, D = x.shape
    L = ada_w.shape[0]
    T = B * S
    tm = TOKEN_TILE
    assert S % tm == 0 and tm % SGU_CHUNK == 0 and D % 256 == 0
    assert router_w.shape[2] == N_EXPERTS
    alpha = float((2 * L) ** 0.25)
    tiles_per_seq = S // tm
    nt = T // tm
    n_blocks = (T * TOP_K + nt * N_EXPERTS * (SUBLANES - 1)) // EXPERT_BLOCK + N_EXPERTS + 1
    total_rows = n_blocks * EXPERT_BLOCK
    row_perm = _expert_of_row(jnp.arange(N_EXPERTS))

    xg = jnp.zeros((total_rows, D // 2), jnp.uint32)
    yg = jnp.zeros((total_rows, D // 2), jnp.uint32)
    ada = _ada_all(c, ada_w, ada_b)
    bf = lambda a: a.astype(_BF16)
    for l in range(L):
        shift1, scale1, gate1, shift2, scale2, gate2 = [
            ada[l, :, k * D:(k + 1) * D].reshape(B, 1, D) for k in range(6)]
        x1 = _mixer(alpha, x, shift1, scale1, gate1, bf(w_in[l]), sgu_ln_g[l], sgu_ln_b[l], sgu_w[l],
                    sgu_b[l].T, conv_w[l], conv_b[l], bf(rg_wa[l]), rg_ba[l], bf(rg_wx[l]), rg_bx[l],
                    rg_lambda[l], bf(w_branch_a[l]), bf(w_branch_b[l]), bf(w_out[l]), ln1_g[l], ln1_b[l])
        x1 = x1.reshape(T, D)
        wr_rows = router_w[l].T[row_perm]
        br_rows = router_b[l][row_perm].reshape(N_EXPERTS, 1)
        slots, wk, cnt = _router(x1, shift2, scale2, wr_rows, br_rows, tiles_per_seq)
        tables, blk_e, nused = _dispatch_tables(cnt[:, 0, :], n_blocks)
        xg = _dispatch(tables, x1, shift2, scale2, slots, xg, tiles_per_seq)
        yg = _experts(blk_e, nused, xg, bf(exp_w1[l]), bf(exp_w3[l]), bf(exp_w2[l]), yg)
        x2 = _combine(alpha, tables, x1, shift2, scale2, gate2, slots.transpose(0, 2, 1), wk.transpose(0, 2, 1),
                      yg, bf(sh_w1[l]), bf(sh_w3[l]), bf(sh_w2[l]), ln2_g[l], ln2_b[l], tiles_per_seq)
        x = x2.reshape(B, S, D)
    return x
```

```python
import functools

import jax
import jax.numpy as jnp
from jax import lax
from jax.experimental import pallas as pl
from jax.experimental.pallas import tpu as pltpu

SGU_CHUNK = 128
SGU_GROUPS = 8
RG_HEADS = 8
CONV_WIDTH = 4
RG_C = 8.0
N_EXPERTS = 64
TOP_K = 8
N_GROUPS = 8
GROUP_SIZE = N_EXPERTS // N_GROUPS
TOPK_GROUPS = 4
ROUTED_SCALE = 2.5
LN_EPS = 1e-5

SUBLANES = 8
TOKEN_TILE = 256
SLOT_ROWS = TOP_K * TOKEN_TILE + N_EXPERTS * SUBLANES
EXPERT_BLOCK = 1024
VMEM_LIMIT = 56 * 1024 * 1024

_F32 = jnp.float32
_BF16 = jnp.bfloat16


_GELU_K1 = 2.0 * (2.0 / 3.141592653589793) ** 0.5
_GELU_K2 = _GELU_K1 * 0.044715


def _gelu(x):
    return x * jax.nn.sigmoid(x * (_GELU_K1 + _GELU_K2 * (x * x)))


def _layer_norm(x, g, b):
    mu = jnp.mean(x, axis=-1, keepdims=True)
    xc = x - mu
    var = jnp.mean(xc * xc, axis=-1, keepdims=True)
    return xc * lax.rsqrt(var + LN_EPS) * g + b


def _pack_halves(v):
    n = v.shape[1] // 2
    lo = lax.bitcast_convert_type(v[:, :n], jnp.uint32)
    hi = lax.bitcast_convert_type(v[:, n:], jnp.uint32)
    return (hi & jnp.uint32(0xFFFF0000)) | (lo >> 16)


def _unpack_halves(u):
    lo = lax.bitcast_convert_type(u << 16, _F32).astype(_BF16)
    hi = lax.bitcast_convert_type(u & jnp.uint32(0xFFFF0000), _F32).astype(_BF16)
    return lo, hi


def _ada_kernel(c_ref, w_ref, b_ref, o_ref):
    c = c_ref[...]
    c_act = c * jax.nn.sigmoid(c)
    o_ref[0] = jnp.dot(c_act, w_ref[0], preferred_element_type=_F32,
                       precision=lax.Precision.HIGHEST) + b_ref[0]


def _ada_all(c, ada_w, ada_b):
    L, D, W = ada_w.shape
    B = c.shape[0]
    nb = W // D
    return pl.pallas_call(
        _ada_kernel,
        grid=(L, nb),
        in_specs=[pl.BlockSpec((B, D), lambda l, n: (0, 0)),
                  pl.BlockSpec((1, D, D), lambda l, n: (l, 0, n)),
                  pl.BlockSpec((1, 1, D), lambda l, n: (l, 0, n))],
        out_specs=pl.BlockSpec((1, B, D), lambda l, n: (l, 0, n)),
        out_shape=jax.ShapeDtypeStruct((L, B, W), _F32),
        name="ada",
    )(c, ada_w, ada_b.reshape(L, 1, W))


def _scan_rows(a, b):
    tm, C = a.shape
    row = lax.broadcasted_iota(jnp.int32, a.shape, 0)
    d = 1
    while d < tm:
        if d < SUBLANES:
            keep = row >= d
            a_sh = jnp.where(keep, pltpu.roll(a, d, axis=0), 1.0)
            b_sh = jnp.where(keep, pltpu.roll(b, d, axis=0), 0.0)
        else:
            a_sh = jnp.concatenate([jnp.ones((d, C), _F32), a[:tm - d]], axis=0)
            b_sh = jnp.concatenate([jnp.zeros((d, C), _F32), b[:tm - d]], axis=0)
        b = b + a * b_sh
        a = a * a_sh
        d *= 2
    return a, b


def _mixer_kernel(alpha, x_ref, shift_ref, scale_ref, gate_ref, win_ref, slg_ref, slb_ref, sw_ref, sbt_ref,
                  cw_ref, cb_ref, wa_ref, ba_ref, wx_ref, bx_ref, lam_ref, wba_ref, wbb_ref, wo_ref,
                  lg_ref, lb_ref, o_ref, rbuf, hprev, sp_ref):
    tm, D = x_ref.shape[1], x_ref.shape[2]
    C = SGU_CHUNK
    si = pl.program_id(1)

    @pl.when(si == 0)
    def _():
        rbuf[0:SUBLANES, :] = jnp.zeros((SUBLANES, D), _F32)
        hprev[...] = jnp.zeros_like(hprev)

    x = x_ref[0]
    h = (x * (1.0 + scale_ref[0]) + shift_ref[0]).astype(_BF16)

    def proj(k):
        return jnp.dot(h, win_ref[:, k * D:(k + 1) * D], preferred_element_type=_F32)

    vn = _layer_norm(_gelu(proj(1)), slg_ref[...], slb_ref[...]).astype(_BF16)
    tri = (lax.broadcasted_iota(jnp.int32, (C, C), 0) >= lax.broadcasted_iota(jnp.int32, (C, C), 1))
    for g in range(SGU_GROUPS):
        wm = jnp.where(tri, sw_ref[g], 0.0).astype(_BF16)
        bias = sbt_ref[:, g:g + 1]
        for cc in range(tm // C):
            blk = jnp.dot(wm, vn[cc * C:(cc + 1) * C, g * C:(g + 1) * C], preferred_element_type=_F32)
            sp_ref[cc * C:(cc + 1) * C, g * C:(g + 1) * C] = blk + bias
    y_a = (_gelu(proj(0)) * sp_ref[...]).astype(_BF16)

    rbuf[SUBLANES:SUBLANES + tm, :] = proj(3)
    r_in = cb_ref[...] + cw_ref[CONV_WIDTH - 1:CONV_WIDTH, :] * rbuf[SUBLANES:SUBLANES + tm, :]
    for dlt in range(1, CONV_WIDTH):
        r_in = r_in + cw_ref[CONV_WIDTH - 1 - dlt:CONV_WIDTH - dlt, :] * rbuf[SUBLANES - dlt:SUBLANES - dlt + tm, :]
    rbuf[0:SUBLANES, :] = rbuf[tm:tm + SUBLANES, :]
    r_bf = r_in.astype(_BF16)
    hd = D // RG_HEADS
    ra = jnp.concatenate([jnp.dot(r_bf[:, k * hd:(k + 1) * hd], wa_ref[k], preferred_element_type=_F32)
                          for k in range(RG_HEADS)], axis=1)
    rx = jnp.concatenate([jnp.dot(r_bf[:, k * hd:(k + 1) * hd], wx_ref[k], preferred_element_type=_F32)
                          for k in range(RG_HEADS)], axis=1)
    r_gate = jax.nn.sigmoid(ra + ba_ref[...])
    i_gate = jax.nn.sigmoid(rx + bx_ref[...])
    neg_lam = -lam_ref[...]
    softplus = jnp.maximum(neg_lam, 0.0) + jnp.log1p(jnp.exp(-jnp.abs(neg_lam)))
    log_a = (-RG_C) * r_gate * softplus
    a = jnp.exp(log_a)
    t = jnp.tanh(log_a)
    inp = jnp.sqrt(-2.0 * t / (1.0 - t)) * (i_gate * r_in)
    a_cum, h_loc = _scan_rows(a, inp)
    h_seq = h_loc + a_cum * hprev[...]
    hprev[...] = h_seq[tm - 1:tm, :]
    y_b = (_gelu(proj(2)) * h_seq).astype(_BF16)

    merged = (jax.nn.sigmoid(proj(4)) * jnp.dot(y_a, wba_ref[...], preferred_element_type=_F32)
              + jax.nn.sigmoid(proj(5)) * jnp.dot(y_b, wbb_ref[...], preferred_element_type=_F32))
    mix = jnp.dot(merged.astype(_BF16), wo_ref[...], preferred_element_type=_F32)
    o_ref[0] = _layer_norm(alpha * x + gate_ref[0] * mix, lg_ref[...], lb_ref[...])


def _const_spec(shape):
    nd = len(shape)
    return pl.BlockSpec(shape, lambda b, s: (0,) * nd, pipeline_mode=pl.Buffered(1))


def _mixer(alpha, x, shift, scale, gate, w_in, slg, slb, sw, sbt, cw, cb, wa, ba, wx, bx, lam, wba, wbb, wo, lg, lb):
    B, S, D = x.shape
    tm = TOKEN_TILE
    vec = pl.BlockSpec((1, 1, D), lambda b, s: (b, 0, 0))
    row = lambda a: a.reshape(1, D)
    consts = [w_in, row(slg), row(slb), sw, sbt, cw, row(cb), wa, row(ba), wx, row(bx), row(lam), wba, wbb, wo,
              row(lg), row(lb)]
    return pl.pallas_call(
        functools.partial(_mixer_kernel, alpha),
        grid=(B, S // tm),
        in_specs=[pl.BlockSpec((1, tm, D), lambda b, s: (b, s, 0)), vec, vec, vec]
                 + [_const_spec(a.shape) for a in consts],
        out_specs=pl.BlockSpec((1, tm, D), lambda b, s: (b, s, 0)),
        out_shape=jax.ShapeDtypeStruct((B, S, D), _F32),
        scratch_shapes=[pltpu.VMEM((tm + SUBLANES, D), _F32), pltpu.VMEM((1, D), _F32), pltpu.VMEM((tm, D), _F32)],
        compiler_params=pltpu.CompilerParams(dimension_semantics=("arbitrary", "arbitrary"),
                                             vmem_limit_bytes=VMEM_LIMIT),
        name="mixer",
    )(x, shift, scale, gate, *consts)


def _expert_of_row(p):
    return (p % N_GROUPS) * GROUP_SIZE + p // N_GROUPS


def _router_kernel(x_ref, shift_ref, scale_ref, wr_ref, br_ref, slot_ref, wk_ref, cnt_ref):
    tm = x_ref.shape[0]
    G, J = N_GROUPS, GROUP_SIZE
    h2 = x_ref[...] * (1.0 + scale_ref[0]) + shift_ref[0]
    logits = lax.dot_general(wr_ref[...], h2, (((1,), (1,)), ((), ())), preferred_element_type=_F32,
                             precision=lax.Precision.HIGHEST)
    scores = jax.nn.sigmoid(logits)
    biased = scores + br_ref[...]
    slabs = [biased[j * G:(j + 1) * G, :] for j in range(J)]

    m1, m2 = slabs[0], jnp.full_like(slabs[0], -jnp.inf)
    for v in slabs[1:]:
        m2 = jnp.maximum(m2, jnp.minimum(m1, v))
        m1 = jnp.maximum(m1, v)
    gs = m1 + m2
    gidx = lax.broadcasted_iota(jnp.int32, gs.shape, 0)
    grank = jnp.zeros(gs.shape, jnp.int32)
    for g2 in range(G):
        other = gs[g2:g2 + 1, :]
        beats = (other > gs) | ((other == gs) & (g2 < gidx))
        grank = grank + beats.astype(jnp.int32)
    keep = grank < TOPK_GROUPS
    masked = [jnp.where(keep, v, -jnp.inf) for v in slabs]

    ranks = [jnp.zeros(gs.shape, jnp.int32) for _ in range(J)]
    for j2 in range(J):
        for g2 in range(G):
            other = masked[j2][g2:g2 + 1, :]
            for j in range(J):
                tie = (gidx > g2) if j2 >= j else (gidx >= g2)
                beats = (other > masked[j]) | ((other == masked[j]) & tie)
                ranks[j] = ranks[j] + beats.astype(jnp.int32)
    sel = jnp.concatenate([(ranks[j] < TOP_K) & keep for j in range(J)], axis=0)
    sel_f = sel.astype(_F32)
    w_sel = scores * sel_f
    wn = w_sel / jnp.sum(w_sel, axis=0, keepdims=True) * ROUTED_SCALE

    sel_b = sel_f.astype(_BF16)
    E = N_EXPERTS
    upper = (lax.broadcasted_iota(jnp.int32, (tm, tm), 0) < lax.broadcasted_iota(jnp.int32, (tm, tm), 1))
    tok_rank = jnp.dot(sel_b, upper.astype(_BF16), preferred_element_type=_F32)
    cnt = jnp.sum(sel_f, axis=1, keepdims=True)
    cnt_tiles = jnp.floor((cnt + (SUBLANES - 1)) * (1.0 / SUBLANES))
    lower = (lax.broadcasted_iota(jnp.int32, (E, E), 1) < lax.broadcasted_iota(jnp.int32, (E, E), 0))
    lower_b = lower.astype(_BF16)
    start = jnp.dot(lower_b, jnp.broadcast_to(cnt_tiles, (E, 128)).astype(_BF16),
                    preferred_element_type=_F32)[:, 0:1] * float(SUBLANES)
    slot = start + tok_rank
    k_of = jnp.dot(lower_b, sel_b, preferred_element_type=_F32)
    slot_rows, w_rows = [], []
    for k in range(TOP_K):
        pick = sel & (k_of == float(k))
        slot_rows.append(jnp.sum(jnp.where(pick, slot, 0.0), axis=0, keepdims=True))
        w_rows.append(jnp.sum(jnp.where(pick, wn, 0.0), axis=0, keepdims=True))
    slot_ref[0] = jnp.concatenate(slot_rows, axis=0).astype(jnp.int32)
    wk_ref[0] = jnp.concatenate(w_rows, axis=0)
    ones = jnp.ones((SUBLANES, tm), _BF16)
    cnt_ref[0] = lax.dot_general(ones, sel_b, (((1,), (1,)), ((), ())), preferred_element_type=_F32)


def _router(x1, shift, scale, wr_rows, br_rows, tiles_per_seq):
    T, D = x1.shape
    tm = TOKEN_TILE
    nt = T // tm
    vec = pl.BlockSpec((1, 1, D), lambda j: (j // tiles_per_seq, 0, 0))
    return pl.pallas_call(
        _router_kernel,
        grid=(nt,),
        in_specs=[pl.BlockSpec((tm, D), lambda j: (j, 0)), vec, vec,
                  pl.BlockSpec((N_EXPERTS, D), lambda j: (0, 0)),
                  pl.BlockSpec((N_EXPERTS, 1), lambda j: (0, 0))],
        out_specs=[pl.BlockSpec((1, TOP_K, tm), lambda j: (j, 0, 0)),
                   pl.BlockSpec((1, TOP_K, tm), lambda j: (j, 0, 0)),
                   pl.BlockSpec((1, SUBLANES, N_EXPERTS), lambda j: (j, 0, 0))],
        out_shape=[jax.ShapeDtypeStruct((nt, TOP_K, tm), jnp.int32),
                   jax.ShapeDtypeStruct((nt, TOP_K, tm), _F32),
                   jax.ShapeDtypeStruct((nt, SUBLANES, N_EXPERTS), _F32)],
        compiler_params=pltpu.CompilerParams(dimension_semantics=("arbitrary",), vmem_limit_bytes=VMEM_LIMIT),
        name="router",
    )(x1, shift, scale, wr_rows, br_rows)


def _segment_copies(cnt_ref, lstart_ref, goff_ref, j, local_ref, global_ref, sem, to_global):
    for p in range(N_EXPERTS):
        n = pl.multiple_of(cnt_ref[j * N_EXPERTS + p], SUBLANES)
        ls = pl.multiple_of(lstart_ref[j * N_EXPERTS + p], SUBLANES)
        go = pl.multiple_of(goff_ref[j * N_EXPERTS + p], SUBLANES)

        @pl.when(n > 0)
        def _():
            loc = local_ref.at[pl.ds(ls, n)]
            glo = global_ref.at[pl.ds(go, n)]
            if to_global:
                pltpu.make_async_copy(loc, glo, sem).start()
            else:
                pltpu.make_async_copy(glo, loc, sem).start()


def _wait_rows(tot_ref, j, local_ref, global_ref, sem, to_global):
    n = pl.multiple_of(tot_ref[j], SUBLANES)

    @pl.when(n > 0)
    def _():
        loc = local_ref.at[pl.ds(0, n)]
        glo = global_ref.at[pl.ds(0, n)]
        if to_global:
            pltpu.make_async_copy(loc, glo, sem).wait()
        else:
            pltpu.make_async_copy(glo, loc, sem).wait()


def _dispatch_kernel(cnt_ref, lstart_ref, goff_ref, tot_ref, x_ref, shift_ref, scale_ref, slot_ref, xg_prev_ref,
                     xg_ref, xs_ref, sem):
    del xg_prev_ref
    j = pl.program_id(0)
    tm, D = x_ref.shape
    h2 = (x_ref[...] * (1.0 + scale_ref[0]) + shift_ref[0]).astype(_BF16)
    slots = slot_ref[0]
    s_iota = lax.broadcasted_iota(jnp.int32, (SLOT_ROWS, tm), 0)
    hit = s_iota == slots[0:1, :]
    for k in range(1, TOP_K):
        hit = hit | (s_iota == slots[k:k + 1, :])
    onehot = jnp.where(hit, 1.0, 0.0).astype(_BF16)
    half = D // 2
    lo = jnp.dot(onehot, h2[:, :half], preferred_element_type=_F32)
    hi = jnp.dot(onehot, h2[:, half:], preferred_element_type=_F32)
    cur = j % 2
    xs_ref[cur] = ((lax.bitcast_convert_type(hi, jnp.uint32) & jnp.uint32(0xFFFF0000))
                   | (lax.bitcast_convert_type(lo, jnp.uint32) >> 16))
    _segment_copies(cnt_ref, lstart_ref, goff_ref, j, xs_ref.at[cur], xg_ref, sem.at[cur], True)

    @pl.when(j > 0)
    def _():
        _wait_rows(tot_ref, j - 1, xs_ref.at[1 - cur], xg_ref, sem.at[1 - cur], True)

    @pl.when(j == pl.num_programs(0) - 1)
    def _():
        _wait_rows(tot_ref, j, xs_ref.at[cur], xg_ref, sem.at[cur], True)


def _dispatch(tables, x1, shift, scale, slots, xg_prev, tiles_per_seq):
    T, D = x1.shape
    tm = TOKEN_TILE
    nt = T // tm
    vec = pl.BlockSpec((1, 1, D), lambda j, *_: (j // tiles_per_seq, 0, 0))
    return pl.pallas_call(
        _dispatch_kernel,
        grid_spec=pltpu.PrefetchScalarGridSpec(
            num_scalar_prefetch=4,
            grid=(nt,),
            in_specs=[pl.BlockSpec((tm, D), lambda j, *_: (j, 0)), vec, vec,
                      pl.BlockSpec((1, TOP_K, tm), lambda j, *_: (j, 0, 0)),
                      pl.BlockSpec(memory_space=pl.ANY)],
            out_specs=pl.BlockSpec(memory_space=pl.ANY),
            scratch_shapes=[pltpu.VMEM((2, SLOT_ROWS, D // 2), jnp.uint32), pltpu.SemaphoreType.DMA((2,))],
        ),
        out_shape=jax.ShapeDtypeStruct(xg_prev.shape, jnp.uint32),
        input_output_aliases={8: 0},
        compiler_params=pltpu.CompilerParams(dimension_semantics=("arbitrary",), vmem_limit_bytes=VMEM_LIMIT),
        name="dispatch",
    )(*tables, x1, shift, scale, slots, xg_prev)


def _expert_kernel(blk_ref, nused_ref, x_ref, w1_ref, w3_ref, w2_ref, y_prev_ref, o_ref):
    del y_prev_ref
    i = pl.program_id(0)

    @pl.when(i < nused_ref[0])
    def _():
        half = w1_ref.shape[1] // 2
        lo, hi = _unpack_halves(x_ref[...])
        a = (jnp.dot(lo, w1_ref[0, :half, :], preferred_element_type=_F32)
             + jnp.dot(hi, w1_ref[0, half:, :], preferred_element_type=_F32))
        b = (jnp.dot(lo, w3_ref[0, :half, :], preferred_element_type=_F32)
             + jnp.dot(hi, w3_ref[0, half:, :], preferred_element_type=_F32))
        hid = (a * jax.nn.sigmoid(a) * b).astype(_BF16)
        y = jnp.dot(hid, w2_ref[0], preferred_element_type=_F32)
        o_ref[...] = _pack_halves(y.astype(_BF16).astype(_F32))


def _experts(blk_e, nused, xg, w1, w3, w2, yg_prev):
    R, Dh = xg.shape
    bm = EXPERT_BLOCK
    _, D, H = w1.shape
    rows = lambda i, blk, nu: (jnp.minimum(i, nu[0] - 1), 0)
    wsel = lambda i, blk, nu: (blk[i], 0, 0)
    return pl.pallas_call(
        _expert_kernel,
        grid_spec=pltpu.PrefetchScalarGridSpec(
            num_scalar_prefetch=2,
            grid=(R // bm,),
            in_specs=[pl.BlockSpec((bm, Dh), rows),
                      pl.BlockSpec((1, D, H), wsel), pl.BlockSpec((1, D, H), wsel), pl.BlockSpec((1, H, D), wsel),
                      pl.BlockSpec(memory_space=pl.ANY)],
            out_specs=pl.BlockSpec((bm, Dh), rows),
        ),
        out_shape=jax.ShapeDtypeStruct((R, Dh), jnp.uint32),
        input_output_aliases={6: 0},
        compiler_params=pltpu.CompilerParams(dimension_semantics=("arbitrary",), vmem_limit_bytes=VMEM_LIMIT),
        name="experts",
    )(blk_e, nused, xg, w1, w3, w2, yg_prev)


def _combine_kernel(alpha, cnt_ref, lstart_ref, goff_ref, tot_ref, x_ref, shift_ref, scale_ref, gate_ref,
                    slot_ref, wk_ref, yg_ref, s1_ref, s3_ref, s2_ref, lg_ref, lb_ref, o_ref, ys_ref, sem):
    j = pl.program_id(0)
    tm, D = x_ref.shape

    cur = j % 2

    @pl.when(j == 0)
    def _():
        ys_ref[...] = jnp.zeros_like(ys_ref)
        _segment_copies(cnt_ref, lstart_ref, goff_ref, j, ys_ref.at[0], yg_ref, sem.at[0], False)

    @pl.when(j + 1 < pl.num_programs(0))
    def _():
        _segment_copies(cnt_ref, lstart_ref, goff_ref, j + 1, ys_ref.at[1 - cur], yg_ref, sem.at[1 - cur], False)

    x = x_ref[...]
    h2 = (x * (1.0 + scale_ref[0]) + shift_ref[0]).astype(_BF16)
    shared_hid = jnp.dot(h2, s1_ref[...], preferred_element_type=_F32)
    shared_hid = shared_hid * jax.nn.sigmoid(shared_hid) * jnp.dot(h2, s3_ref[...], preferred_element_type=_F32)
    shared = jnp.dot(shared_hid.astype(_BF16), s2_ref[...], preferred_element_type=_F32)

    slots = slot_ref[0]
    wk = wk_ref[0]
    s_iota = lax.broadcasted_iota(jnp.int32, (tm, SLOT_ROWS), 1)
    weights = jnp.zeros((tm, SLOT_ROWS), _F32)
    for k in range(TOP_K):
        weights = jnp.where(s_iota == slots[:, k:k + 1], wk[:, k:k + 1], weights)
    weights = weights.astype(_BF16)

    _wait_rows(tot_ref, j, ys_ref.at[cur], yg_ref, sem.at[cur], False)
    lo, hi = _unpack_halves(ys_ref[cur])
    routed = jnp.concatenate([jnp.dot(weights, lo, preferred_element_type=_F32),
                              jnp.dot(weights, hi, preferred_element_type=_F32)], axis=1)
    o_ref[...] = _layer_norm(alpha * x + gate_ref[0] * (routed + shared), lg_ref[...], lb_ref[...])


def _combine(alpha, tables, x1, shift, scale, gate, slots_t, wk_t, yg, s1, s3, s2, lg, lb, tiles_per_seq):
    T, D = x1.shape
    tm = TOKEN_TILE
    nt = T // tm
    H = s1.shape[1]
    vec = pl.BlockSpec((1, 1, D), lambda j, *_: (j // tiles_per_seq, 0, 0))
    const = lambda shape: pl.BlockSpec(shape, lambda j, *_: (0,) * len(shape))
    return pl.pallas_call(
        functools.partial(_combine_kernel, alpha),
        grid_spec=pltpu.PrefetchScalarGridSpec(
            num_scalar_prefetch=4,
            grid=(nt,),
            in_specs=[pl.BlockSpec((tm, D), lambda j, *_: (j, 0)), vec, vec, vec,
                      pl.BlockSpec((1, tm, TOP_K), lambda j, *_: (j, 0, 0)),
                      pl.BlockSpec((1, tm, TOP_K), lambda j, *_: (j, 0, 0)),
                      pl.BlockSpec(memory_space=pl.ANY),
                      const((D, H)), const((D, H)), const((H, D)), const((1, D)), const((1, D))],
            out_specs=pl.BlockSpec((tm, D), lambda j, *_: (j, 0)),
            scratch_shapes=[pltpu.VMEM((2, SLOT_ROWS, D // 2), jnp.uint32), pltpu.SemaphoreType.DMA((2,))],
        ),
        out_shape=jax.ShapeDtypeStruct((T, D), _F32),
        compiler_params=pltpu.CompilerParams(dimension_semantics=("arbitrary",), vmem_limit_bytes=VMEM_LIMIT),
        name="combine",
    )(*tables, x1, shift, scale, gate, slots_t, wk_t, yg, s1, s3, s2, lg.reshape(1, D), lb.reshape(1, D))


def _round_up(v, m):
    return (v + m - 1) // m * m


def _dispatch_tables(cnt_f, n_blocks):
    cnt = _round_up(cnt_f.astype(jnp.int32), SUBLANES)
    lstart = jnp.cumsum(cnt, axis=1) - cnt
    per_row = jnp.sum(cnt, axis=0)
    region = _round_up(per_row, EXPERT_BLOCK)
    region_end = jnp.cumsum(region)
    goff = (region_end - region)[None, :] + jnp.cumsum(cnt, axis=0) - cnt
    tot = jnp.sum(cnt, axis=1)
    nused = (region_end[-1] // EXPERT_BLOCK).reshape(1)
    blk_first = jnp.arange(n_blocks, dtype=jnp.int32) * EXPERT_BLOCK
    blk_row = jnp.sum((region_end[None, :] <= blk_first[:, None]).astype(jnp.int32), axis=1)
    blk_row = jnp.minimum(blk_row, N_EXPERTS - 1)
    blk_e = _expert_of_row(blk_row)
    tables = (cnt.reshape(-1), lstart.reshape(-1).astype(jnp.int32), goff.reshape(-1).astype(jnp.int32),
              tot.astype(jnp.int32))
    return tables, blk_e.astype(jnp.int32), nused.astype(jnp.int32)


def kernel(x, c, ada_w, ada_b, w_in, sgu_ln_g, sgu_ln_b, sgu_w, sgu_b, conv_w, conv_b, rg_wa, rg_ba, rg_wx, rg_bx, rg_lambda, w_branch_a, w_branch_b, w_out, ln1_g, ln1_b, router_w, router_b, exp_w1, exp_w3, exp_w2, sh_w1, sh_w3, sh_w2, ln2_g, ln2_b):
    B, S, D = x.shape
    L = ada_w.shape[0]
    T = B * S
    tm = TOKEN_TILE
    assert S % tm == 0 and tm % SGU_CHUNK == 0 and D % 256 == 0
    assert router_w.shape[2] == N_EXPERTS
    alpha = float((2 * L) ** 0.25)
    tiles_per_seq = S // tm
    nt = T // tm
    n_blocks = (T * TOP_K + nt * N_EXPERTS * (SUBLANES - 1)) // EXPERT_BLOCK + N_EXPERTS + 1
    total_rows = n_blocks * EXPERT_BLOCK
    row_perm = _expert_of_row(jnp.arange(N_EXPERTS))

    xg = jnp.zeros((total_rows, D // 2), jnp.uint32)
    yg = jnp.zeros((total_rows, D // 2), jnp.uint32)
    ada = _ada_all(c, ada_w, ada_b)
    bf = lambda a: a.astype(_BF16)
    for l in range(L):
        shift1, scale1, gate1, shift2, scale2, gate2 = [
            ada[l, :, k * D:(k + 1) * D].reshape(B, 1, D) for k in range(6)]
        x1 = _mixer(alpha, x, shift1, scale1, gate1, bf(w_in[l]), sgu_ln_g[l], sgu_ln_b[l], sgu_w[l],
                    sgu_b[l].T, conv_w[l], conv_b[l], bf(rg_wa[l]), rg_ba[l], bf(rg_wx[l]), rg_bx[l],
                    rg_lambda[l], bf(w_branch_a[l]), bf(w_branch_b[l]), bf(w_out[l]), ln1_g[l], ln1_b[l])
        x1 = x1.reshape(T, D)
        wr_rows = router_w[l].T[row_perm]
        br_rows = router_b[l][row_perm].reshape(N_EXPERTS, 1)
        slots, wk, cnt = _router(x1, shift2, scale2, wr_rows, br_rows, tiles_per_seq)
        tables, blk_e, nused = _dispatch_tables(cnt[:, 0, :], n_blocks)
        xg = _dispatch(tables, x1, shift2, scale2, slots, xg, tiles_per_seq)
        yg = _experts(blk_e, nused, xg, bf(exp_w1[l]), bf(exp_w3[l]), bf(exp_w2[l]), yg)
        x2 = _combine(alpha, tables, x1, shift2, scale2, gate2, slots.transpose(0, 2, 1), wk.transpose(0, 2, 1),
                      yg, bf(sh_w1[l]), bf(sh_w3[l]), bf(sh_w2[l]), ln2_g[l], ln2_b[l], tiles_per_seq)
        x = x2.reshape(B, S, D)
    return x
```

```python
import functools

import jax
import jax.numpy as jnp
from jax import lax
from jax.experimental import pallas as pl
from jax.experimental.pallas import tpu as pltpu

SGU_CHUNK = 128
SGU_GROUPS = 8
RG_HEADS = 8
CONV_WIDTH = 4
RG_C = 8.0
N_EXPERTS = 64
TOP_K = 8
N_GROUPS = 8
GROUP_SIZE = N_EXPERTS // N_GROUPS
TOPK_GROUPS = 4
ROUTED_SCALE = 2.5
LN_EPS = 1e-5

SUBLANES = 8
TOKEN_TILE = 256
SLOT_ROWS = TOP_K * TOKEN_TILE + N_EXPERTS * SUBLANES
EXPERT_BLOCK = 1024
SLOT_CHUNK = 256
SLOT_RADIX = 64.0
SLOT_NONE = 255.0
VMEM_LIMIT = 56 * 1024 * 1024

_F32 = jnp.float32
_BF16 = jnp.bfloat16


_GELU_K1 = 2.0 * (2.0 / 3.141592653589793) ** 0.5
_GELU_K2 = _GELU_K1 * 0.044715


def _gelu(x):
    return x * jax.nn.sigmoid(x * (_GELU_K1 + _GELU_K2 * (x * x)))


def _layer_norm(x, g, b):
    mu = jnp.mean(x, axis=-1, keepdims=True)
    xc = x - mu
    var = jnp.mean(xc * xc, axis=-1, keepdims=True)
    return xc * lax.rsqrt(var + LN_EPS) * g + b


def _pack_halves(v):
    n = v.shape[1] // 2
    lo = lax.bitcast_convert_type(v[:, :n], jnp.uint32)
    hi = lax.bitcast_convert_type(v[:, n:], jnp.uint32)
    return (hi & jnp.uint32(0xFFFF0000)) | (lo >> 16)


def _unpack_halves(u):
    lo = lax.bitcast_convert_type(u << 16, _F32).astype(_BF16)
    hi = lax.bitcast_convert_type(u & jnp.uint32(0xFFFF0000), _F32).astype(_BF16)
    return lo, hi


def _ada_kernel(c_ref, w_ref, b_ref, o_ref):
    c = c_ref[...]
    c_act = c * jax.nn.sigmoid(c)
    o_ref[0] = jnp.dot(c_act, w_ref[0], preferred_element_type=_F32,
                       precision=lax.Precision.HIGHEST) + b_ref[0]


def _ada_all(c, ada_w, ada_b):
    L, D, W = ada_w.shape
    B = c.shape[0]
    nb = W // D
    return pl.pallas_call(
        _ada_kernel,
        grid=(L, nb),
        in_specs=[pl.BlockSpec((B, D), lambda l, n: (0, 0)),
                  pl.BlockSpec((1, D, D), lambda l, n: (l, 0, n)),
                  pl.BlockSpec((1, 1, D), lambda l, n: (l, 0, n))],
        out_specs=pl.BlockSpec((1, B, D), lambda l, n: (l, 0, n)),
        out_shape=jax.ShapeDtypeStruct((L, B, W), _F32),
        name="ada",
    )(c, ada_w, ada_b.reshape(L, 1, W))


def _scan_rows(a, b):
    tm, C = a.shape
    row = lax.broadcasted_iota(jnp.int32, a.shape, 0)
    d = 1
    while d < tm:
        if d < SUBLANES:
            keep = row >= d
            a_sh = jnp.where(keep, pltpu.roll(a, d, axis=0), 1.0)
            b_sh = jnp.where(keep, pltpu.roll(b, d, axis=0), 0.0)
        else:
            a_sh = jnp.concatenate([jnp.ones((d, C), _F32), a[:tm - d]], axis=0)
            b_sh = jnp.concatenate([jnp.zeros((d, C), _F32), b[:tm - d]], axis=0)
        b = b + a * b_sh
        a = a * a_sh
        d *= 2
    return a, b


def _mixer_kernel(alpha, x_ref, shift_ref, scale_ref, gate_ref, win_ref, slg_ref, slb_ref, sw_ref, sbt_ref,
                  cw_ref, cb_ref, wa_ref, ba_ref, wx_ref, bx_ref, lam_ref, wba_ref, wbb_ref, wo_ref,
                  lg_ref, lb_ref, o_ref, rbuf, hprev, sp_ref):
    tm, D = x_ref.shape[1], x_ref.shape[2]
    C = SGU_CHUNK
    si = pl.program_id(1)

    @pl.when(si == 0)
    def _():
        rbuf[0:SUBLANES, :] = jnp.zeros((SUBLANES, D), _F32)
        hprev[...] = jnp.zeros_like(hprev)

    x = x_ref[0]
    h = (x * (1.0 + scale_ref[0]) + shift_ref[0]).astype(_BF16)

    def proj(k):
        return jnp.dot(h, win_ref[:, k * D:(k + 1) * D], preferred_element_type=_F32)

    vn = _layer_norm(_gelu(proj(1)), slg_ref[...], slb_ref[...]).astype(_BF16)
    tri = (lax.broadcasted_iota(jnp.int32, (C, C), 0) >= lax.broadcasted_iota(jnp.int32, (C, C), 1))
    for g in range(SGU_GROUPS):
        wm = jnp.where(tri, sw_ref[g], 0.0).astype(_BF16)
        bias = sbt_ref[:, g:g + 1]
        for cc in range(tm // C):
            blk = jnp.dot(wm, vn[cc * C:(cc + 1) * C, g * C:(g + 1) * C], preferred_element_type=_F32)
            sp_ref[cc * C:(cc + 1) * C, g * C:(g + 1) * C] = blk + bias
    y_a = (_gelu(proj(0)) * sp_ref[...]).astype(_BF16)

    rbuf[SUBLANES:SUBLANES + tm, :] = proj(3)
    r_in = cb_ref[...] + cw_ref[CONV_WIDTH - 1:CONV_WIDTH, :] * rbuf[SUBLANES:SUBLANES + tm, :]
    for dlt in range(1, CONV_WIDTH):
        r_in = r_in + cw_ref[CONV_WIDTH - 1 - dlt:CONV_WIDTH - dlt, :] * rbuf[SUBLANES - dlt:SUBLANES - dlt + tm, :]
    rbuf[0:SUBLANES, :] = rbuf[tm:tm + SUBLANES, :]
    r_bf = r_in.astype(_BF16)
    hd = D // RG_HEADS
    ra = jnp.concatenate([jnp.dot(r_bf[:, k * hd:(k + 1) * hd], wa_ref[k], preferred_element_type=_F32)
                          for k in range(RG_HEADS)], axis=1)
    rx = jnp.concatenate([jnp.dot(r_bf[:, k * hd:(k + 1) * hd], wx_ref[k], preferred_element_type=_F32)
                          for k in range(RG_HEADS)], axis=1)
    r_gate = jax.nn.sigmoid(ra + ba_ref[...])
    i_gate = jax.nn.sigmoid(rx + bx_ref[...])
    neg_lam = -lam_ref[...]
    softplus = jnp.maximum(neg_lam, 0.0) + jnp.log1p(jnp.exp(-jnp.abs(neg_lam)))
    log_a = (-RG_C) * r_gate * softplus
    a = jnp.exp(log_a)
    t = jnp.tanh(log_a)
    inp = jnp.sqrt(-2.0 * t / (1.0 - t)) * (i_gate * r_in)
    a_cum, h_loc = _scan_rows(a, inp)
    h_seq = h_loc + a_cum * hprev[...]
    hprev[...] = h_seq[tm - 1:tm, :]
    y_b = (_gelu(proj(2)) * h_seq).astype(_BF16)

    merged = (jax.nn.sigmoid(proj(4)) * jnp.dot(y_a, wba_ref[...], preferred_element_type=_F32)
              + jax.nn.sigmoid(proj(5)) * jnp.dot(y_b, wbb_ref[...], preferred_element_type=_F32))
    mix = jnp.dot(merged.astype(_BF16), wo_ref[...], preferred_element_type=_F32)
    o_ref[0] = _layer_norm(alpha * x + gate_ref[0] * mix, lg_ref[...], lb_ref[...])


def _const_spec(shape):
    nd = len(shape)
    return pl.BlockSpec(shape, lambda b, s: (0,) * nd, pipeline_mode=pl.Buffered(1))


def _mixer(alpha, x, shift, scale, gate, w_in, slg, slb, sw, sbt, cw, cb, wa, ba, wx, bx, lam, wba, wbb, wo, lg, lb):
    B, S, D = x.shape
    tm = TOKEN_TILE
    vec = pl.BlockSpec((1, 1, D), lambda b, s: (b, 0, 0))
    row = lambda a: a.reshape(1, D)
    consts = [w_in, row(slg), row(slb), sw, sbt, cw, row(cb), wa, row(ba), wx, row(bx), row(lam), wba, wbb, wo,
              row(lg), row(lb)]
    return pl.pallas_call(
        functools.partial(_mixer_kernel, alpha),
        grid=(B, S // tm),
        in_specs=[pl.BlockSpec((1, tm, D), lambda b, s: (b, s, 0)), vec, vec, vec]
                 + [_const_spec(a.shape) for a in consts],
        out_specs=pl.BlockSpec((1, tm, D), lambda b, s: (b, s, 0)),
        out_shape=jax.ShapeDtypeStruct((B, S, D), _F32),
        scratch_shapes=[pltpu.VMEM((tm + SUBLANES, D), _F32), pltpu.VMEM((1, D), _F32), pltpu.VMEM((tm, D), _F32)],
        compiler_params=pltpu.CompilerParams(dimension_semantics=("arbitrary", "arbitrary"),
                                             vmem_limit_bytes=VMEM_LIMIT),
        name="mixer",
    )(x, shift, scale, gate, *consts)


def _expert_of_row(p):
    return (p % N_GROUPS) * GROUP_SIZE + p // N_GROUPS


def _router_kernel(x_ref, shift_ref, scale_ref, wr_ref, br_ref, digits_ref, wn_ref, cnt_ref):
    tm = x_ref.shape[0]
    G, J = N_GROUPS, GROUP_SIZE
    h2 = x_ref[...] * (1.0 + scale_ref[0]) + shift_ref[0]
    logits = lax.dot_general(wr_ref[...], h2, (((1,), (1,)), ((), ())), preferred_element_type=_F32,
                             precision=lax.Precision.HIGHEST)
    scores = jax.nn.sigmoid(logits)
    biased = scores + br_ref[...]
    slabs = [biased[j * G:(j + 1) * G, :] for j in range(J)]

    m1, m2 = slabs[0], jnp.full_like(slabs[0], -jnp.inf)
    for v in slabs[1:]:
        m2 = jnp.maximum(m2, jnp.minimum(m1, v))
        m1 = jnp.maximum(m1, v)
    gs = m1 + m2
    gidx = lax.broadcasted_iota(jnp.int32, gs.shape, 0)
    grank = jnp.zeros(gs.shape, jnp.int32)
    for g2 in range(G):
        other = gs[g2:g2 + 1, :]
        beats = (other > gs) | ((other == gs) & (g2 < gidx))
        grank = grank + beats.astype(jnp.int32)
    keep = grank < TOPK_GROUPS
    masked = [jnp.where(keep, v, -jnp.inf) for v in slabs]

    eidx = [gidx * J + j for j in range(J)]
    picked = [jnp.zeros(gs.shape, jnp.bool_) for _ in range(J)]
    for _ in range(TOP_K):
        m = masked[0]
        for j in range(1, J):
            m = jnp.maximum(m, masked[j])
        top = jnp.max(m, axis=0, keepdims=True)
        cand = jnp.where(masked[0] == top, eidx[0], N_EXPERTS)
        for j in range(1, J):
            cand = jnp.minimum(cand, jnp.where(masked[j] == top, eidx[j], N_EXPERTS))
        first = jnp.min(cand, axis=0, keepdims=True)
        for j in range(J):
            hit = eidx[j] == first
            picked[j] = picked[j] | hit
            masked[j] = jnp.where(hit, -jnp.inf, masked[j])
    sel = jnp.concatenate(picked, axis=0)
    sel_f = sel.astype(_F32)
    w_sel = scores * sel_f
    wn_ref[0] = w_sel / jnp.sum(w_sel, axis=0, keepdims=True) * ROUTED_SCALE

    sel_b = sel_f.astype(_BF16)
    E = N_EXPERTS
    upper = (lax.broadcasted_iota(jnp.int32, (tm, tm), 0) < lax.broadcasted_iota(jnp.int32, (tm, tm), 1))
    tok_rank = jnp.dot(sel_b, upper.astype(_BF16), preferred_element_type=_F32)
    cnt = jnp.sum(sel_f, axis=1, keepdims=True)
    cnt_tiles = jnp.floor((cnt + (SUBLANES - 1)) * (1.0 / SUBLANES))
    lower = (lax.broadcasted_iota(jnp.int32, (E, E), 1) < lax.broadcasted_iota(jnp.int32, (E, E), 0))
    start = jnp.dot(lower.astype(_BF16), jnp.broadcast_to(cnt_tiles, (E, 128)).astype(_BF16),
                    preferred_element_type=_F32)[:, 0:1] * float(SUBLANES)
    slot1 = start + tok_rank + 1.0
    hi = jnp.floor(slot1 * (1.0 / SLOT_RADIX))
    lo = slot1 - hi * SLOT_RADIX
    digits_ref[0] = jnp.concatenate([jnp.where(sel, hi, SLOT_NONE), jnp.where(sel, lo, 0.0)], axis=1).astype(_BF16)
    ones = jnp.ones((SUBLANES, tm), _BF16)
    cnt_ref[0] = lax.dot_general(ones, sel_b, (((1,), (1,)), ((), ())), preferred_element_type=_F32)


def _router(x1, shift, scale, wr_rows, br_rows, tiles_per_seq):
    T, D = x1.shape
    tm = TOKEN_TILE
    nt = T // tm
    vec = pl.BlockSpec((1, 1, D), lambda j: (j // tiles_per_seq, 0, 0))
    return pl.pallas_call(
        _router_kernel,
        grid=(nt,),
        in_specs=[pl.BlockSpec((tm, D), lambda j: (j, 0)), vec, vec,
                  pl.BlockSpec((N_EXPERTS, D), lambda j: (0, 0)),
                  pl.BlockSpec((N_EXPERTS, 1), lambda j: (0, 0))],
        out_specs=[pl.BlockSpec((1, N_EXPERTS, 2 * tm), lambda j: (j, 0, 0)),
                   pl.BlockSpec((1, N_EXPERTS, tm), lambda j: (j, 0, 0)),
                   pl.BlockSpec((1, SUBLANES, N_EXPERTS), lambda j: (j, 0, 0))],
        out_shape=[jax.ShapeDtypeStruct((nt, N_EXPERTS, 2 * tm), _BF16),
                   jax.ShapeDtypeStruct((nt, N_EXPERTS, tm), _F32),
                   jax.ShapeDtypeStruct((nt, SUBLANES, N_EXPERTS), _F32)],
        compiler_params=pltpu.CompilerParams(dimension_semantics=("arbitrary",), vmem_limit_bytes=VMEM_LIMIT),
        name="router",
    )(x1, shift, scale, wr_rows, br_rows)


def _segment_copies(cnt_ref, lstart_ref, goff_ref, j, local_ref, global_ref, sem, to_global):
    for p in range(N_EXPERTS):
        n = pl.multiple_of(cnt_ref[j * N_EXPERTS + p], SUBLANES)
        ls = pl.multiple_of(lstart_ref[j * N_EXPERTS + p], SUBLANES)
        go = pl.multiple_of(goff_ref[j * N_EXPERTS + p], SUBLANES)

        @pl.when(n > 0)
        def _():
            loc = local_ref.at[pl.ds(ls, n)]
            glo = global_ref.at[pl.ds(go, n)]
            if to_global:
                pltpu.make_async_copy(loc, glo, sem).start()
            else:
                pltpu.make_async_copy(glo, loc, sem).start()


def _wait_rows(tot_ref, j, local_ref, global_ref, sem, to_global):
    n = pl.multiple_of(tot_ref[j], SUBLANES)

    @pl.when(n > 0)
    def _():
        loc = local_ref.at[pl.ds(0, n)]
        glo = global_ref.at[pl.ds(0, n)]
        if to_global:
            pltpu.make_async_copy(loc, glo, sem).wait()
        else:
            pltpu.make_async_copy(glo, loc, sem).wait()


def _dispatch_kernel(cnt_ref, lstart_ref, goff_ref, tot_ref, x_ref, shift_ref, scale_ref, digits_ref, seg_ref,
                     xg_prev_ref, xg_ref, xs_ref, sem):
    del xg_prev_ref
    j = pl.program_id(0)
    tm, D = x_ref.shape
    cur = j % 2
    h2 = (x_ref[...] * (1.0 + scale_ref[0]) + shift_ref[0]).astype(_BF16)
    digits = digits_ref[0]
    seg_lo, seg_hi = seg_ref[0, 0:1, :], seg_ref[0, 1:2, :]
    for c in range(SLOT_ROWS // SLOT_CHUNK):
        base = float(c * SLOT_CHUNK)
        rows_e = lax.broadcasted_iota(jnp.int32, (SLOT_CHUNK, N_EXPERTS), 0).astype(_F32) + base
        owner = jnp.where((rows_e >= seg_lo) & (rows_e < seg_hi), 1.0, 0.0).astype(_BF16)
        dig = jnp.dot(owner, digits, preferred_element_type=_F32)
        rows_t = lax.broadcasted_iota(jnp.int32, (SLOT_CHUNK, tm), 0).astype(_F32) + (base + 1.0)
        onehot = jnp.where(dig[:, :tm] * SLOT_RADIX + dig[:, tm:] == rows_t, 1.0, 0.0).astype(_BF16)
        picked = jnp.dot(onehot, h2, preferred_element_type=_F32)
        xs_ref[cur, c * SLOT_CHUNK:(c + 1) * SLOT_CHUNK, :] = _pack_halves(picked)
    _segment_copies(cnt_ref, lstart_ref, goff_ref, j, xs_ref.at[cur], xg_ref, sem.at[cur], True)

    @pl.when(j > 0)
    def _():
        _wait_rows(tot_ref, j - 1, xs_ref.at[1 - cur], xg_ref, sem.at[1 - cur], True)

    @pl.when(j == pl.num_programs(0) - 1)
    def _():
        _wait_rows(tot_ref, j, xs_ref.at[cur], xg_ref, sem.at[cur], True)


def _dispatch(tables, x1, shift, scale, digits, seg_rows, xg_prev, tiles_per_seq):
    T, D = x1.shape
    tm = TOKEN_TILE
    nt = T // tm
    vec = pl.BlockSpec((1, 1, D), lambda j, *_: (j // tiles_per_seq, 0, 0))
    return pl.pallas_call(
        _dispatch_kernel,
        grid_spec=pltpu.PrefetchScalarGridSpec(
            num_scalar_prefetch=4,
            grid=(nt,),
            in_specs=[pl.BlockSpec((tm, D), lambda j, *_: (j, 0)), vec, vec,
                      pl.BlockSpec((1, N_EXPERTS, 2 * tm), lambda j, *_: (j, 0, 0)),
                      pl.BlockSpec((1, 2, N_EXPERTS), lambda j, *_: (j, 0, 0)),
                      pl.BlockSpec(memory_space=pl.ANY)],
            out_specs=pl.BlockSpec(memory_space=pl.ANY),
            scratch_shapes=[pltpu.VMEM((2, SLOT_ROWS, D // 2), jnp.uint32), pltpu.SemaphoreType.DMA((2,))],
        ),
        out_shape=jax.ShapeDtypeStruct(xg_prev.shape, jnp.uint32),
        input_output_aliases={9: 0},
        compiler_params=pltpu.CompilerParams(dimension_semantics=("arbitrary",), vmem_limit_bytes=VMEM_LIMIT),
        name="dispatch",
    )(*tables, x1, shift, scale, digits, seg_rows, xg_prev)


def _expert_kernel(blk_ref, nused_ref, x_ref, w1_ref, w3_ref, w2_ref, y_prev_ref, o_ref):
    del y_prev_ref
    i = pl.program_id(0)

    @pl.when(i < nused_ref[0])
    def _():
        half = w1_ref.shape[1] // 2
        lo, hi = _unpack_halves(x_ref[...])
        a = (jnp.dot(lo, w1_ref[0, :half, :], preferred_element_type=_F32)
             + jnp.dot(hi, w1_ref[0, half:, :], preferred_element_type=_F32))
        b = (jnp.dot(lo, w3_ref[0, :half, :], preferred_element_type=_F32)
             + jnp.dot(hi, w3_ref[0, half:, :], preferred_element_type=_F32))
        hid = (a * jax.nn.sigmoid(a) * b).astype(_BF16)
        y = jnp.dot(hid, w2_ref[0], preferred_element_type=_F32)
        o_ref[...] = _pack_halves(y.astype(_BF16).astype(_F32))


def _experts(blk_e, nused, xg, w1, w3, w2, yg_prev):
    R, Dh = xg.shape
    bm = EXPERT_BLOCK
    _, D, H = w1.shape
    rows = lambda i, blk, nu: (jnp.minimum(i, nu[0] - 1), 0)
    wsel = lambda i, blk, nu: (blk[i], 0, 0)
    return pl.pallas_call(
        _expert_kernel,
        grid_spec=pltpu.PrefetchScalarGridSpec(
            num_scalar_prefetch=2,
            grid=(R // bm,),
            in_specs=[pl.BlockSpec((bm, Dh), rows),
                      pl.BlockSpec((1, D, H), wsel), pl.BlockSpec((1, D, H), wsel), pl.BlockSpec((1, H, D), wsel),
                      pl.BlockSpec(memory_space=pl.ANY)],
            out_specs=pl.BlockSpec((bm, Dh), rows),
        ),
        out_shape=jax.ShapeDtypeStruct((R, Dh), jnp.uint32),
        input_output_aliases={6: 0},
        compiler_params=pltpu.CompilerParams(dimension_semantics=("arbitrary",), vmem_limit_bytes=VMEM_LIMIT),
        name="experts",
    )(blk_e, nused, xg, w1, w3, w2, yg_prev)


def _combine_kernel(alpha, cnt_ref, lstart_ref, goff_ref, tot_ref, x_ref, shift_ref, scale_ref, gate_ref,
                    lhs_ref, seg_ref, yg_ref, s1_ref, s3_ref, s2_ref, lg_ref, lb_ref, o_ref, ys_ref, sem):
    j = pl.program_id(0)
    tm, D = x_ref.shape

    cur = j % 2

    @pl.when(j == 0)
    def _():
        ys_ref[...] = jnp.zeros_like(ys_ref)
        _segment_copies(cnt_ref, lstart_ref, goff_ref, j, ys_ref.at[0], yg_ref, sem.at[0], False)

    @pl.when(j + 1 < pl.num_programs(0))
    def _():
        _segment_copies(cnt_ref, lstart_ref, goff_ref, j + 1, ys_ref.at[1 - cur], yg_ref, sem.at[1 - cur], False)

    x = x_ref[...]
    h2 = (x * (1.0 + scale_ref[0]) + shift_ref[0]).astype(_BF16)
    shared_hid = jnp.dot(h2, s1_ref[...], preferred_element_type=_F32)
    shared_hid = shared_hid * jax.nn.sigmoid(shared_hid) * jnp.dot(h2, s3_ref[...], preferred_element_type=_F32)
    acc = jnp.dot(shared_hid.astype(_BF16), s2_ref[...], preferred_element_type=_F32)

    lhs = lhs_ref[0]
    seg_lo, seg_hi = seg_ref[0, :, 0:1], seg_ref[0, :, 1:2]
    _wait_rows(tot_ref, j, ys_ref.at[cur], yg_ref, sem.at[cur], False)
    half = D // 2
    acc_lo, acc_hi = acc[:, :half], acc[:, half:]
    for c in range(SLOT_ROWS // SLOT_CHUNK):
        base = float(c * SLOT_CHUNK)
        rows_e = lax.broadcasted_iota(jnp.int32, (N_EXPERTS, SLOT_CHUNK), 1).astype(_F32) + base
        owner = jnp.where((rows_e >= seg_lo) & (rows_e < seg_hi), 1.0, 0.0).astype(_BF16)
        info = jnp.dot(lhs, owner, preferred_element_type=_F32)
        rows_t = lax.broadcasted_iota(jnp.int32, (tm, SLOT_CHUNK), 1).astype(_F32) + (base + 1.0)
        match = info[:tm] * SLOT_RADIX + info[tm:2 * tm] == rows_t
        weights = jnp.where(match, info[2 * tm:], 0.0).astype(_BF16)
        lo, hi = _unpack_halves(ys_ref[cur, c * SLOT_CHUNK:(c + 1) * SLOT_CHUNK, :])
        acc_lo = acc_lo + jnp.dot(weights, lo, preferred_element_type=_F32)
        acc_hi = acc_hi + jnp.dot(weights, hi, preferred_element_type=_F32)
    moe = jnp.concatenate([acc_lo, acc_hi], axis=1)
    o_ref[...] = _layer_norm(alpha * x + gate_ref[0] * moe, lg_ref[...], lb_ref[...])


def _combine(alpha, tables, x1, shift, scale, gate, lhs, seg_cols, yg, s1, s3, s2, lg, lb, tiles_per_seq):
    T, D = x1.shape
    tm = TOKEN_TILE
    nt = T // tm
    H = s1.shape[1]
    vec = pl.BlockSpec((1, 1, D), lambda j, *_: (j // tiles_per_seq, 0, 0))
    const = lambda shape: pl.BlockSpec(shape, lambda j, *_: (0,) * len(shape))
    return pl.pallas_call(
        functools.partial(_combine_kernel, alpha),
        grid_spec=pltpu.PrefetchScalarGridSpec(
            num_scalar_prefetch=4,
            grid=(nt,),
            in_specs=[pl.BlockSpec((tm, D), lambda j, *_: (j, 0)), vec, vec, vec,
                      pl.BlockSpec((1, 3 * tm, N_EXPERTS), lambda j, *_: (j, 0, 0)),
                      pl.BlockSpec((1, N_EXPERTS, 2), lambda j, *_: (j, 0, 0)),
                      pl.BlockSpec(memory_space=pl.ANY),
                      const((D, H)), const((D, H)), const((H, D)), const((1, D)), const((1, D))],
            out_specs=pl.BlockSpec((tm, D), lambda j, *_: (j, 0)),
            scratch_shapes=[pltpu.VMEM((2, SLOT_ROWS, D // 2), jnp.uint32), pltpu.SemaphoreType.DMA((2,))],
        ),
        out_shape=jax.ShapeDtypeStruct((T, D), _F32),
        compiler_params=pltpu.CompilerParams(dimension_semantics=("arbitrary",), vmem_limit_bytes=VMEM_LIMIT),
        name="combine",
    )(*tables, x1, shift, scale, gate, lhs, seg_cols, yg, s1, s3, s2, lg.reshape(1, D), lb.reshape(1, D))


def _round_up(v, m):
    return (v + m - 1) // m * m


def _dispatch_tables(cnt_f, n_blocks):
    cnt_true = cnt_f.astype(jnp.int32)
    cnt = _round_up(cnt_true, SUBLANES)
    lstart = jnp.cumsum(cnt, axis=1) - cnt
    seg = jnp.stack([lstart, lstart + cnt_true], axis=1).astype(_F32)
    per_row = jnp.sum(cnt, axis=0)
    region = _round_up(per_row, EXPERT_BLOCK)
    region_end = jnp.cumsum(region)
    goff = (region_end - region)[None, :] + jnp.cumsum(cnt, axis=0) - cnt
    tot = jnp.sum(cnt, axis=1)
    nused = (region_end[-1] // EXPERT_BLOCK).reshape(1)
    blk_first = jnp.arange(n_blocks, dtype=jnp.int32) * EXPERT_BLOCK
    blk_row = jnp.sum((region_end[None, :] <= blk_first[:, None]).astype(jnp.int32), axis=1)
    blk_row = jnp.minimum(blk_row, N_EXPERTS - 1)
    blk_e = _expert_of_row(blk_row)
    tables = (cnt.reshape(-1), lstart.reshape(-1).astype(jnp.int32), goff.reshape(-1).astype(jnp.int32),
              tot.astype(jnp.int32))
    return tables, seg, blk_e.astype(jnp.int32), nused.astype(jnp.int32)


def kernel(x, c, ada_w, ada_b, w_in, sgu_ln_g, sgu_ln_b, sgu_w, sgu_b, conv_w, conv_b, rg_wa, rg_ba, rg_wx, rg_bx, rg_lambda, w_branch_a, w_branch_b, w_out, ln1_g, ln1_b, router_w, router_b, exp_w1, exp_w3, exp_w2, sh_w1, sh_w3, sh_w2, ln2_g, ln2_b):
    B, S, D = x.shape
    L = ada_w.shape[0]
    T = B * S
    tm = TOKEN_TILE
    assert S % tm == 0 and tm % SGU_CHUNK == 0 and D % 256 == 0
    assert router_w.shape[2] == N_EXPERTS
    alpha = float((2 * L) ** 0.25)
    tiles_per_seq = S // tm
    nt = T // tm
    n_blocks = (T * TOP_K + nt * N_EXPERTS * (SUBLANES - 1)) // EXPERT_BLOCK + N_EXPERTS + 1
    total_rows = n_blocks * EXPERT_BLOCK
    row_perm = _expert_of_row(jnp.arange(N_EXPERTS))

    xg = jnp.zeros((total_rows, D // 2), jnp.uint32)
    yg = jnp.zeros((total_rows, D // 2), jnp.uint32)
    ada = _ada_all(c, ada_w, ada_b)
    bf = lambda a: a.astype(_BF16)
    for l in range(L):
        shift1, scale1, gate1, shift2, scale2, gate2 = [
            ada[l, :, k * D:(k + 1) * D].reshape(B, 1, D) for k in range(6)]
        x1 = _mixer(alpha, x, shift1, scale1, gate1, bf(w_in[l]), sgu_ln_g[l], sgu_ln_b[l], sgu_w[l],
                    sgu_b[l].T, conv_w[l], conv_b[l], bf(rg_wa[l]), rg_ba[l], bf(rg_wx[l]), rg_bx[l],
                    rg_lambda[l], bf(w_branch_a[l]), bf(w_branch_b[l]), bf(w_out[l]), ln1_g[l], ln1_b[l])
        x1 = x1.reshape(T, D)
        wr_rows = router_w[l].T[row_perm]
        br_rows = router_b[l][row_perm].reshape(N_EXPERTS, 1)
        digits, wn, cnt = _router(x1, shift2, scale2, wr_rows, br_rows, tiles_per_seq)
        tables, seg, blk_e, nused = _dispatch_tables(cnt[:, 0, :], n_blocks)
        xg = _dispatch(tables, x1, shift2, scale2, digits, seg, xg, tiles_per_seq)
        lhs = jnp.concatenate([digits.reshape(nt, N_EXPERTS, 2, tm).transpose(0, 2, 3, 1).reshape(nt, 2 * tm, N_EXPERTS),
                               wn.transpose(0, 2, 1).astype(_BF16)], axis=1)
        yg = _experts(blk_e, nused, xg, bf(exp_w1[l]), bf(exp_w3[l]), bf(exp_w2[l]), yg)
        x2 = _combine(alpha, tables, x1, shift2, scale2, gate2, lhs, seg.transpose(0, 2, 1),
                      yg, bf(sh_w1[l]), bf(sh_w3[l]), bf(sh_w2[l]), ln2_g[l], ln2_b[l], tiles_per_seq)
        x = x2.reshape(B, S, D)
    return x
```

```python
import functools

import jax
import jax.numpy as jnp
from jax import lax
from jax.experimental import pallas as pl
from jax.experimental.pallas import tpu as pltpu

SGU_CHUNK = 128
SGU_GROUPS = 8
RG_HEADS = 8
CONV_WIDTH = 4
RG_C = 8.0
N_EXPERTS = 64
TOP_K = 8
N_GROUPS = 8
GROUP_SIZE = N_EXPERTS // N_GROUPS
TOPK_GROUPS = 4
ROUTED_SCALE = 2.5
LN_EPS = 1e-5

SUBLANES = 8
TOKEN_TILE = 256
SLOT_ROWS = TOP_K * TOKEN_TILE + N_EXPERTS * SUBLANES
EXPERT_BLOCK = 1024
SLOT_CHUNK = 256
SLOT_RADIX = 64.0
SLOT_NONE = 255.0
OWNER_LANES = 2 * N_EXPERTS
ROW_WORDS_EXTRA = 128
HEAD_PACK = 2
VMEM_LIMIT = 56 * 1024 * 1024

_F32 = jnp.float32
_BF16 = jnp.bfloat16


_GELU_K1 = 2.0 * (2.0 / 3.141592653589793) ** 0.5
_GELU_K2 = _GELU_K1 * 0.044715


def _gelu(x):
    return x * jax.nn.sigmoid(x * (_GELU_K1 + _GELU_K2 * (x * x)))


def _layer_norm(x, g, b):
    mu = jnp.mean(x, axis=-1, keepdims=True)
    xc = x - mu
    var = jnp.mean(xc * xc, axis=-1, keepdims=True)
    return xc * lax.rsqrt(var + LN_EPS) * g + b


def _pack_halves(v):
    n = v.shape[1] // 2
    lo = lax.bitcast_convert_type(v[:, :n], jnp.uint32)
    hi = lax.bitcast_convert_type(v[:, n:], jnp.uint32)
    return (hi & jnp.uint32(0xFFFF0000)) | (lo >> 16)


def _unpack_halves(u):
    lo = lax.bitcast_convert_type(u << 16, _F32).astype(_BF16)
    hi = lax.bitcast_convert_type(u & jnp.uint32(0xFFFF0000), _F32).astype(_BF16)
    return lo, hi


def _ada_kernel(c_ref, w_ref, b_ref, o_ref):
    c = c_ref[...]
    c_act = c * jax.nn.sigmoid(c)
    o_ref[0] = jnp.dot(c_act, w_ref[0], preferred_element_type=_F32,
                       precision=lax.Precision.HIGHEST) + b_ref[0]


def _ada_all(c, ada_w, ada_b):
    L, D, W = ada_w.shape
    B = c.shape[0]
    nb = W // D
    return pl.pallas_call(
        _ada_kernel,
        grid=(L, nb),
        in_specs=[pl.BlockSpec((B, D), lambda l, n: (0, 0)),
                  pl.BlockSpec((1, D, D), lambda l, n: (l, 0, n)),
                  pl.BlockSpec((1, 1, D), lambda l, n: (l, 0, n))],
        out_specs=pl.BlockSpec((1, B, D), lambda l, n: (l, 0, n)),
        out_shape=jax.ShapeDtypeStruct((L, B, W), _F32),
        name="ada",
    )(c, ada_w, ada_b.reshape(L, 1, W))


def _scan_rows(a, b):
    tm, C = a.shape
    row = lax.broadcasted_iota(jnp.int32, a.shape, 0)
    d = 1
    while d < tm:
        if d < SUBLANES:
            keep = row >= d
            a_sh = jnp.where(keep, pltpu.roll(a, d, axis=0), 1.0)
            b_sh = jnp.where(keep, pltpu.roll(b, d, axis=0), 0.0)
        else:
            a_sh = jnp.concatenate([jnp.ones((d, C), _F32), a[:tm - d]], axis=0)
            b_sh = jnp.concatenate([jnp.zeros((d, C), _F32), b[:tm - d]], axis=0)
        b = b + a * b_sh
        a = a * a_sh
        d *= 2
    return a, b


def _mixer_kernel(alpha, x_ref, shift_ref, scale_ref, gate_ref, win_ref, slg_ref, slb_ref, sw_ref, sbt_ref,
                  cw_ref, cb_ref, wa_ref, ba_ref, wx_ref, bx_ref, lam_ref, wba_ref, wbb_ref, wo_ref,
                  lg_ref, lb_ref, o_ref, rbuf, hprev, sp_ref):
    tm, D = x_ref.shape[1], x_ref.shape[2]
    C = SGU_CHUNK
    si = pl.program_id(1)

    @pl.when(si == 0)
    def _():
        rbuf[0:SUBLANES, :] = jnp.zeros((SUBLANES, D), _F32)
        hprev[...] = jnp.zeros_like(hprev)

    x = x_ref[0]
    h = (x * (1.0 + scale_ref[0]) + shift_ref[0]).astype(_BF16)

    def proj(k):
        return jnp.dot(h, win_ref[:, k * D:(k + 1) * D], preferred_element_type=_F32)

    vn = _layer_norm(_gelu(proj(1)), slg_ref[...], slb_ref[...]).astype(_BF16)
    tri = (lax.broadcasted_iota(jnp.int32, (C, C), 0) >= lax.broadcasted_iota(jnp.int32, (C, C), 1))
    for g in range(SGU_GROUPS):
        wm = jnp.where(tri, sw_ref[g], 0.0).astype(_BF16)
        bias = sbt_ref[:, g:g + 1]
        for cc in range(tm // C):
            blk = jnp.dot(wm, vn[cc * C:(cc + 1) * C, g * C:(g + 1) * C], preferred_element_type=_F32)
            sp_ref[cc * C:(cc + 1) * C, g * C:(g + 1) * C] = blk + bias
    y_a = (_gelu(proj(0)) * sp_ref[...]).astype(_BF16)

    rbuf[SUBLANES:SUBLANES + tm, :] = proj(3)
    r_in = cb_ref[...] + cw_ref[CONV_WIDTH - 1:CONV_WIDTH, :] * rbuf[SUBLANES:SUBLANES + tm, :]
    for dlt in range(1, CONV_WIDTH):
        r_in = r_in + cw_ref[CONV_WIDTH - 1 - dlt:CONV_WIDTH - dlt, :] * rbuf[SUBLANES - dlt:SUBLANES - dlt + tm, :]
    rbuf[0:SUBLANES, :] = rbuf[tm:tm + SUBLANES, :]
    r_bf = r_in.astype(_BF16)
    hd = wa_ref.shape[1]
    ra = jnp.concatenate([jnp.dot(r_bf[:, k * hd:(k + 1) * hd], wa_ref[k], preferred_element_type=_F32)
                          for k in range(D // hd)], axis=1)
    rx = jnp.concatenate([jnp.dot(r_bf[:, k * hd:(k + 1) * hd], wx_ref[k], preferred_element_type=_F32)
                          for k in range(D // hd)], axis=1)
    r_gate = jax.nn.sigmoid(ra + ba_ref[...])
    i_gate = jax.nn.sigmoid(rx + bx_ref[...])
    neg_lam = -lam_ref[...]
    softplus = jnp.maximum(neg_lam, 0.0) + jnp.log1p(jnp.exp(-jnp.abs(neg_lam)))
    log_a = (-RG_C) * r_gate * softplus
    a = jnp.exp(log_a)
    t = jnp.tanh(log_a)
    inp = jnp.sqrt(-2.0 * t / (1.0 - t)) * (i_gate * r_in)
    a_cum, h_loc = _scan_rows(a, inp)
    h_seq = h_loc + a_cum * hprev[...]
    hprev[...] = h_seq[tm - 1:tm, :]
    y_b = (_gelu(proj(2)) * h_seq).astype(_BF16)

    merged = (jax.nn.sigmoid(proj(4)) * jnp.dot(y_a, wba_ref[...], preferred_element_type=_F32)
              + jax.nn.sigmoid(proj(5)) * jnp.dot(y_b, wbb_ref[...], preferred_element_type=_F32))
    mix = jnp.dot(merged.astype(_BF16), wo_ref[...], preferred_element_type=_F32)
    o_ref[0] = _layer_norm(alpha * x + gate_ref[0] * mix, lg_ref[...], lb_ref[...])


def _const_spec(shape):
    nd = len(shape)
    return pl.BlockSpec(shape, lambda b, s: (0,) * nd, pipeline_mode=pl.Buffered(1))


def _mixer(alpha, x, shift, scale, gate, w_in, slg, slb, sw, sbt, cw, cb, wa, ba, wx, bx, lam, wba, wbb, wo, lg, lb):
    B, S, D = x.shape
    tm = TOKEN_TILE
    vec = pl.BlockSpec((1, 1, D), lambda b, s: (b, 0, 0))
    row = lambda a: a.reshape(1, D)
    consts = [w_in, row(slg), row(slb), sw, sbt, cw, row(cb), wa, row(ba), wx, row(bx), row(lam), wba, wbb, wo,
              row(lg), row(lb)]
    return pl.pallas_call(
        functools.partial(_mixer_kernel, alpha),
        grid=(B, S // tm),
        in_specs=[pl.BlockSpec((1, tm, D), lambda b, s: (b, s, 0)), vec, vec, vec]
                 + [_const_spec(a.shape) for a in consts],
        out_specs=pl.BlockSpec((1, tm, D), lambda b, s: (b, s, 0)),
        out_shape=jax.ShapeDtypeStruct((B, S, D), _F32),
        scratch_shapes=[pltpu.VMEM((tm + SUBLANES, D), _F32), pltpu.VMEM((1, D), _F32), pltpu.VMEM((tm, D), _F32)],
        compiler_params=pltpu.CompilerParams(dimension_semantics=("arbitrary", "arbitrary"),
                                             vmem_limit_bytes=VMEM_LIMIT),
        name="mixer",
    )(x, shift, scale, gate, *consts)


def _expert_of_row(p):
    return (p % N_GROUPS) * GROUP_SIZE + p // N_GROUPS


def _router_kernel(x_ref, shift_ref, scale_ref, wr_ref, br_ref, digits_ref, wn_ref, cnt_ref):
    tm = x_ref.shape[0]
    G, J = N_GROUPS, GROUP_SIZE
    h2 = x_ref[...] * (1.0 + scale_ref[0]) + shift_ref[0]
    logits = lax.dot_general(wr_ref[...], h2, (((1,), (1,)), ((), ())), preferred_element_type=_F32,
                             precision=lax.Precision.HIGHEST)
    scores = jax.nn.sigmoid(logits)
    biased = scores + br_ref[...]
    slabs = [biased[j * G:(j + 1) * G, :] for j in range(J)]

    m1, m2 = slabs[0], jnp.full_like(slabs[0], -jnp.inf)
    for v in slabs[1:]:
        m2 = jnp.maximum(m2, jnp.minimum(m1, v))
        m1 = jnp.maximum(m1, v)
    gs = m1 + m2
    gidx = lax.broadcasted_iota(jnp.int32, gs.shape, 0)
    grank = jnp.zeros(gs.shape, jnp.int32)
    for g2 in range(G):
        other = gs[g2:g2 + 1, :]
        beats = (other > gs) | ((other == gs) & (g2 < gidx))
        grank = grank + beats.astype(jnp.int32)
    keep = grank < TOPK_GROUPS
    masked = [jnp.where(keep, v, -jnp.inf) for v in slabs]

    eidx = [gidx * J + j for j in range(J)]
    picked = [jnp.zeros(gs.shape, jnp.bool_) for _ in range(J)]
    for _ in range(TOP_K):
        m = masked[0]
        for j in range(1, J):
            m = jnp.maximum(m, masked[j])
        top = jnp.max(m, axis=0, keepdims=True)
        cand = jnp.where(masked[0] == top, eidx[0], N_EXPERTS)
        for j in range(1, J):
            cand = jnp.minimum(cand, jnp.where(masked[j] == top, eidx[j], N_EXPERTS))
        first = jnp.min(cand, axis=0, keepdims=True)
        for j in range(J):
            hit = eidx[j] == first
            picked[j] = picked[j] | hit
            masked[j] = jnp.where(hit, -jnp.inf, masked[j])
    sel = jnp.concatenate(picked, axis=0)
    sel_f = sel.astype(_F32)
    w_sel = scores * sel_f
    wn_ref[0] = w_sel / jnp.sum(w_sel, axis=0, keepdims=True) * ROUTED_SCALE

    sel_b = sel_f.astype(_BF16)
    E = N_EXPERTS
    upper = (lax.broadcasted_iota(jnp.int32, (tm, tm), 0) < lax.broadcasted_iota(jnp.int32, (tm, tm), 1))
    tok_rank = jnp.dot(sel_b, upper.astype(_BF16), preferred_element_type=_F32)
    cnt = jnp.sum(sel_f, axis=1, keepdims=True)
    cnt_tiles = jnp.floor((cnt + (SUBLANES - 1)) * (1.0 / SUBLANES))
    lower = (lax.broadcasted_iota(jnp.int32, (E, E), 1) < lax.broadcasted_iota(jnp.int32, (E, E), 0))
    start = jnp.dot(lower.astype(_BF16), jnp.broadcast_to(cnt_tiles, (E, 128)).astype(_BF16),
                    preferred_element_type=_F32)[:, 0:1] * float(SUBLANES)
    slot1 = start + tok_rank + 1.0
    hi = jnp.floor(slot1 * (1.0 / SLOT_RADIX))
    lo = slot1 - hi * SLOT_RADIX
    digits = jnp.concatenate([jnp.where(sel, hi, SLOT_NONE), jnp.where(sel, lo, 0.0)], axis=1)
    digits_ref[0] = jnp.concatenate([digits, jnp.zeros_like(digits)], axis=0).astype(_BF16)
    ones = jnp.ones((SUBLANES, tm), _BF16)
    cnt_ref[0] = lax.dot_general(ones, sel_b, (((1,), (1,)), ((), ())), preferred_element_type=_F32)


def _router(x1, shift, scale, wr_rows, br_rows, tiles_per_seq):
    T, D = x1.shape
    tm = TOKEN_TILE
    nt = T // tm
    vec = pl.BlockSpec((1, 1, D), lambda j: (j // tiles_per_seq, 0, 0))
    return pl.pallas_call(
        _router_kernel,
        grid=(nt,),
        in_specs=[pl.BlockSpec((tm, D), lambda j: (j, 0)), vec, vec,
                  pl.BlockSpec((N_EXPERTS, D), lambda j: (0, 0)),
                  pl.BlockSpec((N_EXPERTS, 1), lambda j: (0, 0))],
        out_specs=[pl.BlockSpec((1, OWNER_LANES, 2 * tm), lambda j: (j, 0, 0)),
                   pl.BlockSpec((1, N_EXPERTS, tm), lambda j: (j, 0, 0)),
                   pl.BlockSpec((1, SUBLANES, N_EXPERTS), lambda j: (j, 0, 0))],
        out_shape=[jax.ShapeDtypeStruct((nt, OWNER_LANES, 2 * tm), _BF16),
                   jax.ShapeDtypeStruct((nt, N_EXPERTS, tm), _F32),
                   jax.ShapeDtypeStruct((nt, SUBLANES, N_EXPERTS), _F32)],
        compiler_params=pltpu.CompilerParams(dimension_semantics=("arbitrary",), vmem_limit_bytes=VMEM_LIMIT),
        name="router",
    )(x1, shift, scale, wr_rows, br_rows)


def _segment_copies(cnt_ref, lstart_ref, goff_ref, j, local_ref, global_ref, sem, to_global):
    for p in range(N_EXPERTS):
        n = pl.multiple_of(cnt_ref[j * N_EXPERTS + p], SUBLANES)
        ls = pl.multiple_of(lstart_ref[j * N_EXPERTS + p], SUBLANES)
        go = pl.multiple_of(goff_ref[j * N_EXPERTS + p], SUBLANES)

        @pl.when(n > 0)
        def _():
            loc = local_ref.at[pl.ds(ls, n)]
            glo = global_ref.at[pl.ds(go, n)]
            if to_global:
                pltpu.make_async_copy(loc, glo, sem).start()
            else:
                pltpu.make_async_copy(glo, loc, sem).start()


def _wait_rows(tot_ref, j, local_ref, global_ref, sem, to_global):
    n = pl.multiple_of(tot_ref[j], SUBLANES)

    @pl.when(n > 0)
    def _():
        loc = local_ref.at[pl.ds(0, n)]
        glo = global_ref.at[pl.ds(0, n)]
        if to_global:
            pltpu.make_async_copy(loc, glo, sem).wait()
        else:
            pltpu.make_async_copy(glo, loc, sem).wait()


def _dispatch_kernel(cnt_ref, lstart_ref, goff_ref, tot_ref, x_ref, shift_ref, scale_ref, digits_ref, wn_ref, seg_ref,
                     xg_prev_ref, xg_ref, p_ref, xs_ref, sem):
    del xg_prev_ref
    j = pl.program_id(0)
    tm, D = x_ref.shape
    E = N_EXPERTS
    cur = j % 2
    h2 = (x_ref[...] * (1.0 + scale_ref[0]) + shift_ref[0]).astype(_BF16)
    wn = wn_ref[0]
    w1 = wn.astype(_BF16)
    r1 = wn - w1.astype(_F32)
    w2 = r1.astype(_BF16)
    w3 = (r1 - w2.astype(_F32)).astype(_BF16)
    w12 = (w1.astype(_F32) + pltpu.roll(w2.astype(_F32), E, axis=1)).astype(_BF16)
    src = jnp.concatenate([h2, w12, w3], axis=1)
    digits = digits_ref[0]
    seg_lo, seg_hi = seg_ref[0, 0:1, :], seg_ref[0, 1:2, :]
    for c in range(SLOT_ROWS // SLOT_CHUNK):
        rows = slice(c * SLOT_CHUNK, (c + 1) * SLOT_CHUNK)
        base = float(c * SLOT_CHUNK)
        rows_e = lax.broadcasted_iota(jnp.int32, (SLOT_CHUNK, OWNER_LANES), 0).astype(_F32) + base
        owner = jnp.where((rows_e >= seg_lo) & (rows_e < seg_hi), 1.0, 0.0)
        dig = jnp.dot(owner.astype(_BF16), digits, preferred_element_type=_F32)
        rows_t = lax.broadcasted_iota(jnp.int32, (SLOT_CHUNK, tm), 0).astype(_F32) + (base + 1.0)
        onehot = jnp.where(dig[:, :tm] * SLOT_RADIX + dig[:, tm:] == rows_t, 1.0, 0.0).astype(_BF16)
        p_ref[0, rows, :] = onehot
        picked = jnp.dot(onehot, src, preferred_element_type=_F32)
        w_row = jnp.sum(owner * (picked[:, D:D + OWNER_LANES] + picked[:, D + OWNER_LANES:]), axis=1, keepdims=True)
        xs_ref[cur, rows, :D // 2] = _pack_halves(picked[:, :D])
        xs_ref[cur, rows, D // 2:] = lax.bitcast_convert_type(
            jnp.broadcast_to(w_row, (SLOT_CHUNK, ROW_WORDS_EXTRA)), jnp.uint32)
    _segment_copies(cnt_ref, lstart_ref, goff_ref, j, xs_ref.at[cur], xg_ref, sem.at[cur], True)

    @pl.when(j > 0)
    def _():
        _wait_rows(tot_ref, j - 1, xs_ref.at[1 - cur], xg_ref, sem.at[1 - cur], True)

    @pl.when(j == pl.num_programs(0) - 1)
    def _():
        _wait_rows(tot_ref, j, xs_ref.at[cur], xg_ref, sem.at[cur], True)


def _dispatch(tables, x1, shift, scale, digits, wn_t, seg_rows, xg_prev, tiles_per_seq):
    T, D = x1.shape
    tm = TOKEN_TILE
    nt = T // tm
    vec = pl.BlockSpec((1, 1, D), lambda j, *_: (j // tiles_per_seq, 0, 0))
    return pl.pallas_call(
        _dispatch_kernel,
        grid_spec=pltpu.PrefetchScalarGridSpec(
            num_scalar_prefetch=4,
            grid=(nt,),
            in_specs=[pl.BlockSpec((tm, D), lambda j, *_: (j, 0)), vec, vec,
                      pl.BlockSpec((1, OWNER_LANES, 2 * tm), lambda j, *_: (j, 0, 0)),
                      pl.BlockSpec((1, tm, OWNER_LANES), lambda j, *_: (j, 0, 0)),
                      pl.BlockSpec((1, 2, OWNER_LANES), lambda j, *_: (j, 0, 0)),
                      pl.BlockSpec(memory_space=pl.ANY)],
            out_specs=[pl.BlockSpec(memory_space=pl.ANY),
                       pl.BlockSpec((1, SLOT_ROWS, tm), lambda j, *_: (j, 0, 0))],
            scratch_shapes=[pltpu.VMEM((2, SLOT_ROWS, D // 2 + ROW_WORDS_EXTRA), jnp.uint32),
                            pltpu.SemaphoreType.DMA((2,))],
        ),
        out_shape=[jax.ShapeDtypeStruct(xg_prev.shape, jnp.uint32),
                   jax.ShapeDtypeStruct((nt, SLOT_ROWS, tm), _BF16)],
        input_output_aliases={10: 0},
        compiler_params=pltpu.CompilerParams(dimension_semantics=("arbitrary",), vmem_limit_bytes=VMEM_LIMIT),
        name="dispatch",
    )(*tables, x1, shift, scale, digits, wn_t, seg_rows, xg_prev)


def _expert_kernel(blk_ref, nused_ref, x_ref, w1_ref, w3_ref, w2_ref, y_prev_ref, o_ref):
    del y_prev_ref
    i = pl.program_id(0)

    @pl.when(i < nused_ref[0])
    def _():
        half = w1_ref.shape[1] // 2
        lo, hi = _unpack_halves(x_ref[:, :half])
        w_row = lax.bitcast_convert_type(x_ref[:, half:], _F32)
        a = (jnp.dot(lo, w1_ref[0, :half, :], preferred_element_type=_F32)
             + jnp.dot(hi, w1_ref[0, half:, :], preferred_element_type=_F32))
        b = (jnp.dot(lo, w3_ref[0, :half, :], preferred_element_type=_F32)
             + jnp.dot(hi, w3_ref[0, half:, :], preferred_element_type=_F32))
        hid = (a * jax.nn.sigmoid(a) * b).astype(_BF16)
        y = jnp.dot(hid, w2_ref[0], preferred_element_type=_F32)
        y = y * jnp.tile(w_row, (1, y.shape[1] // w_row.shape[1]))
        o_ref[...] = _pack_halves(y.astype(_BF16).astype(_F32))


def _experts(blk_e, nused, xg, w1, w3, w2, yg_prev):
    R, Dx = xg.shape
    bm = EXPERT_BLOCK
    _, D, H = w1.shape
    Dh = D // 2
    rows = lambda i, blk, nu: (jnp.minimum(i, nu[0] - 1), 0)
    wsel = lambda i, blk, nu: (blk[i], 0, 0)
    return pl.pallas_call(
        _expert_kernel,
        grid_spec=pltpu.PrefetchScalarGridSpec(
            num_scalar_prefetch=2,
            grid=(R // bm,),
            in_specs=[pl.BlockSpec((bm, Dx), rows),
                      pl.BlockSpec((1, D, H), wsel), pl.BlockSpec((1, D, H), wsel), pl.BlockSpec((1, H, D), wsel),
                      pl.BlockSpec(memory_space=pl.ANY)],
            out_specs=pl.BlockSpec((bm, Dh), rows),
        ),
        out_shape=jax.ShapeDtypeStruct((R, Dh), jnp.uint32),
        input_output_aliases={6: 0},
        compiler_params=pltpu.CompilerParams(dimension_semantics=("arbitrary",), vmem_limit_bytes=VMEM_LIMIT),
        name="experts",
    )(blk_e, nused, xg, w1, w3, w2, yg_prev)


def _combine_kernel(alpha, cnt_ref, lstart_ref, goff_ref, tot_ref, x_ref, shift_ref, scale_ref, gate_ref,
                    p_ref, yg_ref, s1_ref, s3_ref, s2_ref, lg_ref, lb_ref, o_ref, ys_ref, sem):
    j = pl.program_id(0)
    cur = j % 2

    @pl.when(j == 0)
    def _():
        ys_ref[...] = jnp.zeros_like(ys_ref)
        _segment_copies(cnt_ref, lstart_ref, goff_ref, j, ys_ref.at[0], yg_ref, sem.at[0], False)

    @pl.when(j + 1 < pl.num_programs(0))
    def _():
        _segment_copies(cnt_ref, lstart_ref, goff_ref, j + 1, ys_ref.at[1 - cur], yg_ref, sem.at[1 - cur], False)

    x = x_ref[...]
    h2 = (x * (1.0 + scale_ref[0]) + shift_ref[0]).astype(_BF16)
    shared_hid = jnp.dot(h2, s1_ref[...], preferred_element_type=_F32)
    shared_hid = shared_hid * jax.nn.sigmoid(shared_hid) * jnp.dot(h2, s3_ref[...], preferred_element_type=_F32)
    shared = jnp.dot(shared_hid.astype(_BF16), s2_ref[...], preferred_element_type=_F32)

    _wait_rows(tot_ref, j, ys_ref.at[cur], yg_ref, sem.at[cur], False)
    onehot = p_ref[0]
    lo, hi = _unpack_halves(ys_ref[cur])
    sum_rows = (((0,), (0,)), ((), ()))
    routed = jnp.concatenate([lax.dot_general(onehot, lo, sum_rows, preferred_element_type=_F32),
                              lax.dot_general(onehot, hi, sum_rows, preferred_element_type=_F32)], axis=1)
    o_ref[...] = _layer_norm(alpha * x + gate_ref[0] * (routed + shared), lg_ref[...], lb_ref[...])


def _combine(alpha, tables, x1, shift, scale, gate, onehot, yg, s1, s3, s2, lg, lb, tiles_per_seq):
    T, D = x1.shape
    tm = TOKEN_TILE
    nt = T // tm
    H = s1.shape[1]
    vec = pl.BlockSpec((1, 1, D), lambda j, *_: (j // tiles_per_seq, 0, 0))
    const = lambda shape: pl.BlockSpec(shape, lambda j, *_: (0,) * len(shape))
    return pl.pallas_call(
        functools.partial(_combine_kernel, alpha),
        grid_spec=pltpu.PrefetchScalarGridSpec(
            num_scalar_prefetch=4,
            grid=(nt,),
            in_specs=[pl.BlockSpec((tm, D), lambda j, *_: (j, 0)), vec, vec, vec,
                      pl.BlockSpec((1, SLOT_ROWS, tm), lambda j, *_: (j, 0, 0)),
                      pl.BlockSpec(memory_space=pl.ANY),
                      const((D, H)), const((D, H)), const((H, D)), const((1, D)), const((1, D))],
            out_specs=pl.BlockSpec((tm, D), lambda j, *_: (j, 0)),
            scratch_shapes=[pltpu.VMEM((2, SLOT_ROWS, D // 2), jnp.uint32), pltpu.SemaphoreType.DMA((2,))],
        ),
        out_shape=jax.ShapeDtypeStruct((T, D), _F32),
        compiler_params=pltpu.CompilerParams(dimension_semantics=("arbitrary",), vmem_limit_bytes=VMEM_LIMIT),
        name="combine",
    )(*tables, x1, shift, scale, gate, onehot, yg, s1, s3, s2, lg.reshape(1, D), lb.reshape(1, D))


def _pack_heads(w):
    H, d, _ = w.shape
    eye = jnp.eye(HEAD_PACK, dtype=w.dtype)
    packed = jnp.einsum("kiab,ij->kiajb", w.reshape(H // HEAD_PACK, HEAD_PACK, d, d), eye)
    return packed.reshape(H // HEAD_PACK, HEAD_PACK * d, HEAD_PACK * d)


def _round_up(v, m):
    return (v + m - 1) // m * m


def _dispatch_tables(cnt_f, n_blocks):
    cnt_true = cnt_f.astype(jnp.int32)
    cnt = _round_up(cnt_true, SUBLANES)
    lstart = jnp.cumsum(cnt, axis=1) - cnt
    seg = jnp.stack([lstart, lstart + cnt_true], axis=1).astype(_F32)
    seg = jnp.concatenate([seg, seg], axis=2)
    per_row = jnp.sum(cnt, axis=0)
    region = _round_up(per_row, EXPERT_BLOCK)
    region_end = jnp.cumsum(region)
    goff = (region_end - region)[None, :] + jnp.cumsum(cnt, axis=0) - cnt
    tot = jnp.sum(cnt, axis=1)
    nused = (region_end[-1] // EXPERT_BLOCK).reshape(1)
    blk_first = jnp.arange(n_blocks, dtype=jnp.int32) * EXPERT_BLOCK
    blk_row = jnp.sum((region_end[None, :] <= blk_first[:, None]).astype(jnp.int32), axis=1)
    blk_row = jnp.minimum(blk_row, N_EXPERTS - 1)
    blk_e = _expert_of_row(blk_row)
    tables = (cnt.reshape(-1), lstart.reshape(-1).astype(jnp.int32), goff.reshape(-1).astype(jnp.int32),
              tot.astype(jnp.int32))
    return tables, seg, blk_e.astype(jnp.int32), nused.astype(jnp.int32)


def kernel(x, c, ada_w, ada_b, w_in, sgu_ln_g, sgu_ln_b, sgu_w, sgu_b, conv_w, conv_b, rg_wa, rg_ba, rg_wx, rg_bx, rg_lambda, w_branch_a, w_branch_b, w_out, ln1_g, ln1_b, router_w, router_b, exp_w1, exp_w3, exp_w2, sh_w1, sh_w3, sh_w2, ln2_g, ln2_b):
    B, S, D = x.shape
    L = ada_w.shape[0]
    T = B * S
    tm = TOKEN_TILE
    assert S % tm == 0 and tm % SGU_CHUNK == 0 and D % 256 == 0
    assert router_w.shape[2] == N_EXPERTS
    alpha = float((2 * L) ** 0.25)
    tiles_per_seq = S // tm
    nt = T // tm
    n_blocks = (T * TOP_K + nt * N_EXPERTS * (SUBLANES - 1)) // EXPERT_BLOCK + N_EXPERTS + 1
    total_rows = n_blocks * EXPERT_BLOCK
    row_perm = _expert_of_row(jnp.arange(N_EXPERTS))

    xg = jnp.zeros((total_rows, D // 2 + ROW_WORDS_EXTRA), jnp.uint32)
    yg = jnp.zeros((total_rows, D // 2), jnp.uint32)
    ada = _ada_all(c, ada_w, ada_b)
    bf = lambda a: a.astype(_BF16)
    for l in range(L):
        shift1, scale1, gate1, shift2, scale2, gate2 = [
            ada[l, :, k * D:(k + 1) * D].reshape(B, 1, D) for k in range(6)]
        x1 = _mixer(alpha, x, shift1, scale1, gate1, bf(w_in[l]), sgu_ln_g[l], sgu_ln_b[l], sgu_w[l],
                    sgu_b[l].T, conv_w[l], conv_b[l], bf(_pack_heads(rg_wa[l])), rg_ba[l], bf(_pack_heads(rg_wx[l])), rg_bx[l],
                    rg_lambda[l], bf(w_branch_a[l]), bf(w_branch_b[l]), bf(w_out[l]), ln1_g[l], ln1_b[l])
        x1 = x1.reshape(T, D)
        wr_rows = router_w[l].T[row_perm]
        br_rows = router_b[l][row_perm].reshape(N_EXPERTS, 1)
        digits, wn, cnt = _router(x1, shift2, scale2, wr_rows, br_rows, tiles_per_seq)
        tables, seg, blk_e, nused = _dispatch_tables(cnt[:, 0, :], n_blocks)
        wn_t = jnp.pad(wn.transpose(0, 2, 1), ((0, 0), (0, 0), (0, N_EXPERTS)))
        xg, onehot = _dispatch(tables, x1, shift2, scale2, digits, wn_t, seg, xg, tiles_per_seq)
        yg = _experts(blk_e, nused, xg, bf(exp_w1[l]), bf(exp_w3[l]), bf(exp_w2[l]), yg)
        x2 = _combine(alpha, tables, x1, shift2, scale2, gate2, onehot,
                      yg, bf(sh_w1[l]), bf(sh_w3[l]), bf(sh_w2[l]), ln2_g[l], ln2_b[l], tiles_per_seq)
        x = x2.reshape(B, S, D)
    return x
```

```python
import functools

import jax
import jax.numpy as jnp
from jax import lax
from jax.experimental import pallas as pl
from jax.experimental.pallas import tpu as pltpu

SGU_CHUNK = 128
SGU_GROUPS = 8
RG_HEADS = 8
CONV_WIDTH = 4
RG_C = 8.0
N_EXPERTS = 64
TOP_K = 8
N_GROUPS = 8
GROUP_SIZE = N_EXPERTS // N_GROUPS
TOPK_GROUPS = 4
ROUTED_SCALE = 2.5
LN_EPS = 1e-5

SUBLANES = 8
TOKEN_TILE = 256
MIXER_TILE = 512
SLOT_ROWS = TOP_K * TOKEN_TILE + N_EXPERTS * SUBLANES
EXPERT_BLOCK = 1024
SLOT_CHUNK = 256
SLOT_RADIX = 64.0
SLOT_NONE = 255.0
OWNER_LANES = 2 * N_EXPERTS
ROW_WORDS_EXTRA = 128
HEAD_PACK = 2
VMEM_LIMIT = 56 * 1024 * 1024

_F32 = jnp.float32
_BF16 = jnp.bfloat16


_GELU_K1 = 2.0 * (2.0 / 3.141592653589793) ** 0.5
_GELU_K2 = _GELU_K1 * 0.044715


def _sigmoid(x):
    return 0.5 * jnp.tanh(0.5 * x) + 0.5


def _gelu(x):
    half = 0.5 * x
    return half + half * jnp.tanh(x * (0.5 * _GELU_K1 + 0.5 * _GELU_K2 * (x * x)))


def _layer_norm(x, g, b):
    mu = jnp.mean(x, axis=-1, keepdims=True)
    xc = x - mu
    var = jnp.mean(xc * xc, axis=-1, keepdims=True)
    return xc * lax.rsqrt(var + LN_EPS) * g + b


def _pack_halves(v):
    n = v.shape[1] // 2
    lo = lax.bitcast_convert_type(v[:, :n], jnp.uint32)
    hi = lax.bitcast_convert_type(v[:, n:], jnp.uint32)
    return (hi & jnp.uint32(0xFFFF0000)) | (lo >> 16)


def _unpack_halves(u):
    lo = lax.bitcast_convert_type(u << 16, _F32).astype(_BF16)
    hi = lax.bitcast_convert_type(u & jnp.uint32(0xFFFF0000), _F32).astype(_BF16)
    return lo, hi


def _ada_kernel(c_ref, w_ref, b_ref, o_ref):
    c = c_ref[...]
    c_act = c * jax.nn.sigmoid(c)
    o_ref[0] = jnp.dot(c_act, w_ref[0], preferred_element_type=_F32,
                       precision=lax.Precision.HIGHEST) + b_ref[0]


def _ada_all(c, ada_w, ada_b):
    L, D, W = ada_w.shape
    B = c.shape[0]
    nb = W // D
    return pl.pallas_call(
        _ada_kernel,
        grid=(L, nb),
        in_specs=[pl.BlockSpec((B, D), lambda l, n: (0, 0)),
                  pl.BlockSpec((1, D, D), lambda l, n: (l, 0, n)),
                  pl.BlockSpec((1, 1, D), lambda l, n: (l, 0, n))],
        out_specs=pl.BlockSpec((1, B, D), lambda l, n: (l, 0, n)),
        out_shape=jax.ShapeDtypeStruct((L, B, W), _F32),
        name="ada",
    )(c, ada_w, ada_b.reshape(L, 1, W))


def _scan_rows(a, b, carry):
    tm, C = a.shape
    sub = lax.broadcasted_iota(jnp.int32, (SUBLANES, C), 0)
    steps = [(d, sub >= d) for d in (1, 2, 4)]
    tiles = []
    for k in range(tm // SUBLANES):
        rows = slice(k * SUBLANES, (k + 1) * SUBLANES)
        ak, bk = a[rows], b[rows]
        for d, keep in steps:
            a_sh = jnp.where(keep, pltpu.roll(ak, d, axis=0), 1.0)
            b_sh = jnp.where(keep, pltpu.roll(bk, d, axis=0), 0.0)
            bk = bk + ak * b_sh
            ak = ak * a_sh
        h = bk + ak * carry
        carry = h[SUBLANES - 1:SUBLANES]
        tiles.append(h)
    return jnp.concatenate(tiles, axis=0)


def _mixer_kernel(alpha, x_ref, shift_ref, scale_ref, gate_ref, win_ref, slg_ref, slb_ref, sw_ref, sbt_ref,
                  cw_ref, cb_ref, wa_ref, ba_ref, wx_ref, bx_ref, lam_ref, wba_ref, wbb_ref, wo_ref,
                  lg_ref, lb_ref, o_ref, rbuf, hprev, sp_ref):
    tm, D = x_ref.shape[1], x_ref.shape[2]
    C = SGU_CHUNK
    si = pl.program_id(1)

    @pl.when(si == 0)
    def _():
        rbuf[0:SUBLANES, :] = jnp.zeros((SUBLANES, D), _F32)
        hprev[...] = jnp.zeros_like(hprev)

    x = x_ref[0]
    h = (x * (1.0 + scale_ref[0]) + shift_ref[0]).astype(_BF16)

    def proj(k):
        return jnp.dot(h, win_ref[:, k * D:(k + 1) * D], preferred_element_type=_F32)

    vn = _layer_norm(_gelu(proj(1)), slg_ref[...], slb_ref[...]).astype(_BF16)
    tri = (lax.broadcasted_iota(jnp.int32, (C, C), 0) >= lax.broadcasted_iota(jnp.int32, (C, C), 1))
    for g in range(SGU_GROUPS):
        wm = jnp.where(tri, sw_ref[g], 0.0).astype(_BF16)
        bias = sbt_ref[:, g:g + 1]
        for cc in range(tm // C):
            blk = jnp.dot(wm, vn[cc * C:(cc + 1) * C, g * C:(g + 1) * C], preferred_element_type=_F32)
            sp_ref[cc * C:(cc + 1) * C, g * C:(g + 1) * C] = blk + bias
    y_a = (_gelu(proj(0)) * sp_ref[...]).astype(_BF16)

    rbuf[SUBLANES:SUBLANES + tm, :] = proj(3)
    r_in = cb_ref[...] + cw_ref[CONV_WIDTH - 1:CONV_WIDTH, :] * rbuf[SUBLANES:SUBLANES + tm, :]
    for dlt in range(1, CONV_WIDTH):
        r_in = r_in + cw_ref[CONV_WIDTH - 1 - dlt:CONV_WIDTH - dlt, :] * rbuf[SUBLANES - dlt:SUBLANES - dlt + tm, :]
    rbuf[0:SUBLANES, :] = rbuf[tm:tm + SUBLANES, :]
    r_bf = r_in.astype(_BF16)
    hd = wa_ref.shape[1]
    ra = jnp.concatenate([jnp.dot(r_bf[:, k * hd:(k + 1) * hd], wa_ref[k], preferred_element_type=_F32)
                          for k in range(D // hd)], axis=1)
    rx = jnp.concatenate([jnp.dot(r_bf[:, k * hd:(k + 1) * hd], wx_ref[k], preferred_element_type=_F32)
                          for k in range(D // hd)], axis=1)
    r_gate = _sigmoid(ra + ba_ref[...])
    i_gate = _sigmoid(rx + bx_ref[...])
    neg_lam = -lam_ref[...]
    softplus = jnp.maximum(neg_lam, 0.0) + jnp.log1p(jnp.exp(-jnp.abs(neg_lam)))
    log_a = (-RG_C) * r_gate * softplus
    a = jnp.exp(log_a)
    t = jnp.tanh(log_a)
    inp = jnp.sqrt(-2.0 * t / (1.0 - t)) * (i_gate * r_in)
    h_seq = _scan_rows(a, inp, hprev[...])
    hprev[...] = h_seq[tm - 1:tm, :]
    y_b = (_gelu(proj(2)) * h_seq).astype(_BF16)

    merged = (_sigmoid(proj(4)) * jnp.dot(y_a, wba_ref[...], preferred_element_type=_F32)
              + _sigmoid(proj(5)) * jnp.dot(y_b, wbb_ref[...], preferred_element_type=_F32))
    mix = jnp.dot(merged.astype(_BF16), wo_ref[...], preferred_element_type=_F32)
    o_ref[0] = _layer_norm(alpha * x + gate_ref[0] * mix, lg_ref[...], lb_ref[...])


def _const_spec(shape):
    nd = len(shape)
    return pl.BlockSpec(shape, lambda b, s: (0,) * nd, pipeline_mode=pl.Buffered(1))


def _mixer(alpha, x, shift, scale, gate, w_in, slg, slb, sw, sbt, cw, cb, wa, ba, wx, bx, lam, wba, wbb, wo, lg, lb):
    B, S, D = x.shape
    tm = MIXER_TILE
    vec = pl.BlockSpec((1, 1, D), lambda b, s: (b, 0, 0))
    row = lambda a: a.reshape(1, D)
    consts = [w_in, row(slg), row(slb), sw, sbt, cw, row(cb), wa, row(ba), wx, row(bx), row(lam), wba, wbb, wo,
              row(lg), row(lb)]
    return pl.pallas_call(
        functools.partial(_mixer_kernel, alpha),
        grid=(B, S // tm),
        in_specs=[pl.BlockSpec((1, tm, D), lambda b, s: (b, s, 0)), vec, vec, vec]
                 + [_const_spec(a.shape) for a in consts],
        out_specs=pl.BlockSpec((1, tm, D), lambda b, s: (b, s, 0)),
        out_shape=jax.ShapeDtypeStruct((B, S, D), _F32),
        scratch_shapes=[pltpu.VMEM((tm + SUBLANES, D), _F32), pltpu.VMEM((1, D), _F32), pltpu.VMEM((tm, D), _F32)],
        compiler_params=pltpu.CompilerParams(dimension_semantics=("arbitrary", "arbitrary"),
                                             vmem_limit_bytes=VMEM_LIMIT),
        name="mixer",
    )(x, shift, scale, gate, *consts)


def _expert_of_row(p):
    return (p % N_GROUPS) * GROUP_SIZE + p // N_GROUPS


def _router_kernel(x_ref, shift_ref, scale_ref, wr_ref, br_ref, digits_ref, wn_ref, cnt_ref):
    tm = x_ref.shape[0]
    G, J = N_GROUPS, GROUP_SIZE
    h2 = x_ref[...] * (1.0 + scale_ref[0]) + shift_ref[0]
    logits = lax.dot_general(wr_ref[...], h2, (((1,), (1,)), ((), ())), preferred_element_type=_F32,
                             precision=lax.Precision.HIGHEST)
    scores = jax.nn.sigmoid(logits)
    biased = scores + br_ref[...]
    slabs = [biased[j * G:(j + 1) * G, :] for j in range(J)]

    m1, m2 = slabs[0], jnp.full_like(slabs[0], -jnp.inf)
    for v in slabs[1:]:
        m2 = jnp.maximum(m2, jnp.minimum(m1, v))
        m1 = jnp.maximum(m1, v)
    gs = m1 + m2
    gidx = lax.broadcasted_iota(jnp.int32, gs.shape, 0)
    grank = jnp.zeros(gs.shape, jnp.int32)
    for g2 in range(G):
        other = gs[g2:g2 + 1, :]
        beats = (other > gs) | ((other == gs) & (g2 < gidx))
        grank = grank + beats.astype(jnp.int32)
    keep = grank < TOPK_GROUPS
    masked = [jnp.where(keep, v, -jnp.inf) for v in slabs]

    eidx = [gidx * J + j for j in range(J)]
    picked = [jnp.zeros(gs.shape, jnp.bool_) for _ in range(J)]
    for _ in range(TOP_K):
        m = masked[0]
        for j in range(1, J):
            m = jnp.maximum(m, masked[j])
        top = jnp.max(m, axis=0, keepdims=True)
        cand = jnp.where(masked[0] == top, eidx[0], N_EXPERTS)
        for j in range(1, J):
            cand = jnp.minimum(cand, jnp.where(masked[j] == top, eidx[j], N_EXPERTS))
        first = jnp.min(cand, axis=0, keepdims=True)
        for j in range(J):
            hit = eidx[j] == first
            picked[j] = picked[j] | hit
            masked[j] = jnp.where(hit, -jnp.inf, masked[j])
    sel = jnp.concatenate(picked, axis=0)
    sel_f = sel.astype(_F32)
    w_sel = scores * sel_f
    wn_ref[0] = w_sel / jnp.sum(w_sel, axis=0, keepdims=True) * ROUTED_SCALE

    sel_b = sel_f.astype(_BF16)
    E = N_EXPERTS
    upper = (lax.broadcasted_iota(jnp.int32, (tm, tm), 0) < lax.broadcasted_iota(jnp.int32, (tm, tm), 1))
    tok_rank = jnp.dot(sel_b, upper.astype(_BF16), preferred_element_type=_F32)
    cnt = jnp.sum(sel_f, axis=1, keepdims=True)
    cnt_tiles = jnp.floor((cnt + (SUBLANES - 1)) * (1.0 / SUBLANES))
    lower = (lax.broadcasted_iota(jnp.int32, (E, E), 1) < lax.broadcasted_iota(jnp.int32, (E, E), 0))
    start = jnp.dot(lower.astype(_BF16), jnp.broadcast_to(cnt_tiles, (E, 128)).astype(_BF16),
                    preferred_element_type=_F32)[:, 0:1] * float(SUBLANES)
    slot1 = start + tok_rank + 1.0
    hi = jnp.floor(slot1 * (1.0 / SLOT_RADIX))
    lo = slot1 - hi * SLOT_RADIX
    digits = jnp.concatenate([jnp.where(sel, hi, SLOT_NONE), jnp.where(sel, lo, 0.0)], axis=1)
    digits_ref[0] = jnp.concatenate([digits, jnp.zeros_like(digits)], axis=0).astype(_BF16)
    ones = jnp.ones((SUBLANES, tm), _BF16)
    cnt_ref[0] = lax.dot_general(ones, sel_b, (((1,), (1,)), ((), ())), preferred_element_type=_F32)


def _router(x1, shift, scale, wr_rows, br_rows, tiles_per_seq):
    T, D = x1.shape
    tm = TOKEN_TILE
    nt = T // tm
    vec = pl.BlockSpec((1, 1, D), lambda j: (j // tiles_per_seq, 0, 0))
    return pl.pallas_call(
        _router_kernel,
        grid=(nt,),
        in_specs=[pl.BlockSpec((tm, D), lambda j: (j, 0)), vec, vec,
                  pl.BlockSpec((N_EXPERTS, D), lambda j: (0, 0)),
                  pl.BlockSpec((N_EXPERTS, 1), lambda j: (0, 0))],
        out_specs=[pl.BlockSpec((1, OWNER_LANES, 2 * tm), lambda j: (j, 0, 0)),
                   pl.BlockSpec((1, N_EXPERTS, tm), lambda j: (j, 0, 0)),
                   pl.BlockSpec((1, SUBLANES, N_EXPERTS), lambda j: (j, 0, 0))],
        out_shape=[jax.ShapeDtypeStruct((nt, OWNER_LANES, 2 * tm), _BF16),
                   jax.ShapeDtypeStruct((nt, N_EXPERTS, tm), _F32),
                   jax.ShapeDtypeStruct((nt, SUBLANES, N_EXPERTS), _F32)],
        compiler_params=pltpu.CompilerParams(dimension_semantics=("arbitrary",), vmem_limit_bytes=VMEM_LIMIT),
        name="router",
    )(x1, shift, scale, wr_rows, br_rows)


def _segment_copies(cnt_ref, lstart_ref, goff_ref, j, local_ref, global_ref, sem, to_global):
    for p in range(N_EXPERTS):
        n = pl.multiple_of(cnt_ref[j * N_EXPERTS + p], SUBLANES)
        ls = pl.multiple_of(lstart_ref[j * N_EXPERTS + p], SUBLANES)
        go = pl.multiple_of(goff_ref[j * N_EXPERTS + p], SUBLANES)

        @pl.when(n > 0)
        def _():
            loc = local_ref.at[pl.ds(ls, n)]
            glo = global_ref.at[pl.ds(go, n), pl.ds(0, local_ref.shape[1])]
            if to_global:
                pltpu.make_async_copy(loc, glo, sem).start()
            else:
                pltpu.make_async_copy(glo, loc, sem).start()


def _wait_rows(tot_ref, j, local_ref, global_ref, sem, to_global):
    n = pl.multiple_of(tot_ref[j], SUBLANES)

    @pl.when(n > 0)
    def _():
        loc = local_ref.at[pl.ds(0, n)]
        glo = global_ref.at[pl.ds(0, n), pl.ds(0, local_ref.shape[1])]
        if to_global:
            pltpu.make_async_copy(loc, glo, sem).wait()
        else:
            pltpu.make_async_copy(glo, loc, sem).wait()


def _dispatch_kernel(cnt_ref, lstart_ref, goff_ref, tot_ref, x_ref, shift_ref, scale_ref, digits_ref, wn_ref, seg_ref,
                     xg_prev_ref, xg_ref, p_ref, xs_ref, sem):
    del xg_prev_ref
    j = pl.program_id(0)
    tm, D = x_ref.shape
    E = N_EXPERTS
    cur = j % 2
    h2 = (x_ref[...] * (1.0 + scale_ref[0]) + shift_ref[0]).astype(_BF16)
    wn = wn_ref[0]
    w1 = wn.astype(_BF16)
    r1 = wn - w1.astype(_F32)
    w2 = r1.astype(_BF16)
    w3 = (r1 - w2.astype(_F32)).astype(_BF16)
    w12 = (w1.astype(_F32) + pltpu.roll(w2.astype(_F32), E, axis=1)).astype(_BF16)
    src = jnp.concatenate([h2, w12, w3], axis=1)
    digits = digits_ref[0]
    seg_lo, seg_hi = seg_ref[0, 0:1, :], seg_ref[0, 1:2, :]
    for c in range(SLOT_ROWS // SLOT_CHUNK):
        rows = slice(c * SLOT_CHUNK, (c + 1) * SLOT_CHUNK)
        base = float(c * SLOT_CHUNK)
        rows_e = lax.broadcasted_iota(jnp.int32, (SLOT_CHUNK, OWNER_LANES), 0).astype(_F32) + base
        owner = jnp.where((rows_e >= seg_lo) & (rows_e < seg_hi), 1.0, 0.0)
        dig = jnp.dot(owner.astype(_BF16), digits, preferred_element_type=_F32)
        rows_t = lax.broadcasted_iota(jnp.int32, (SLOT_CHUNK, tm), 0).astype(_F32) + (base + 1.0)
        onehot = jnp.where(dig[:, :tm] * SLOT_RADIX + dig[:, tm:] == rows_t, 1.0, 0.0).astype(_BF16)
        p_ref[0, rows, :] = onehot
        picked = jnp.dot(onehot, src, preferred_element_type=_F32)
        w_row = jnp.sum(owner * (picked[:, D:D + OWNER_LANES] + picked[:, D + OWNER_LANES:]), axis=1, keepdims=True)
        xs_ref[cur, rows, :D // 2] = _pack_halves(picked[:, :D])
        xs_ref[cur, rows, D // 2:] = lax.bitcast_convert_type(
            jnp.broadcast_to(w_row, (SLOT_CHUNK, ROW_WORDS_EXTRA)), jnp.uint32)
    _segment_copies(cnt_ref, lstart_ref, goff_ref, j, xs_ref.at[cur], xg_ref, sem.at[cur], True)

    @pl.when(j > 0)
    def _():
        _wait_rows(tot_ref, j - 1, xs_ref.at[1 - cur], xg_ref, sem.at[1 - cur], True)

    @pl.when(j == pl.num_programs(0) - 1)
    def _():
        _wait_rows(tot_ref, j, xs_ref.at[cur], xg_ref, sem.at[cur], True)


def _dispatch(tables, x1, shift, scale, digits, wn_t, seg_rows, xg_prev, tiles_per_seq):
    T, D = x1.shape
    tm = TOKEN_TILE
    nt = T // tm
    vec = pl.BlockSpec((1, 1, D), lambda j, *_: (j // tiles_per_seq, 0, 0))
    return pl.pallas_call(
        _dispatch_kernel,
        grid_spec=pltpu.PrefetchScalarGridSpec(
            num_scalar_prefetch=4,
            grid=(nt,),
            in_specs=[pl.BlockSpec((tm, D), lambda j, *_: (j, 0)), vec, vec,
                      pl.BlockSpec((1, OWNER_LANES, 2 * tm), lambda j, *_: (j, 0, 0)),
                      pl.BlockSpec((1, tm, OWNER_LANES), lambda j, *_: (j, 0, 0)),
                      pl.BlockSpec((1, 2, OWNER_LANES), lambda j, *_: (j, 0, 0)),
                      pl.BlockSpec(memory_space=pl.ANY)],
            out_specs=[pl.BlockSpec(memory_space=pl.ANY),
                       pl.BlockSpec((1, SLOT_ROWS, tm), lambda j, *_: (j, 0, 0))],
            scratch_shapes=[pltpu.VMEM((2, SLOT_ROWS, D // 2 + ROW_WORDS_EXTRA), jnp.uint32),
                            pltpu.SemaphoreType.DMA((2,))],
        ),
        out_shape=[jax.ShapeDtypeStruct(xg_prev.shape, jnp.uint32),
                   jax.ShapeDtypeStruct((nt, SLOT_ROWS, tm), _BF16)],
        input_output_aliases={10: 0},
        compiler_params=pltpu.CompilerParams(dimension_semantics=("arbitrary",), vmem_limit_bytes=VMEM_LIMIT),
        name="dispatch",
    )(*tables, x1, shift, scale, digits, wn_t, seg_rows, xg_prev)


def _expert_kernel(blk_ref, nused_ref, x_ref, w1_ref, w3_ref, w2_ref, o_ref):
    i = pl.program_id(0)

    @pl.when(i < nused_ref[0])
    def _():
        half = w1_ref.shape[2] // 2
        lo, hi = _unpack_halves(x_ref[:, :half])
        w_row = lax.bitcast_convert_type(x_ref[:, half:], _F32)
        w1 = w1_ref[0, 0].astype(_BF16)
        w3 = w3_ref[0, 0].astype(_BF16)
        a = (jnp.dot(lo, w1[:half], preferred_element_type=_F32) + jnp.dot(hi, w1[half:], preferred_element_type=_F32))
        b = (jnp.dot(lo, w3[:half], preferred_element_type=_F32) + jnp.dot(hi, w3[half:], preferred_element_type=_F32))
        hid = (a * jax.nn.sigmoid(a) * b).astype(_BF16)
        y = jnp.dot(hid, w2_ref[0, 0].astype(_BF16), preferred_element_type=_F32)
        y = y * jnp.tile(w_row, (1, y.shape[1] // w_row.shape[1]))
        o_ref[...] = _pack_halves(y.astype(_BF16).astype(_F32))


def _experts(layer, blk_e, nused, xg, w1, w3, w2):
    R, Dx = xg.shape
    bm = EXPERT_BLOCK
    _, _, D, H = w1.shape
    rows = lambda i, blk, nu: (jnp.minimum(i, nu[0] - 1), 0)
    wsel = lambda i, blk, nu: (layer, blk[i], 0, 0)
    return pl.pallas_call(
        _expert_kernel,
        grid_spec=pltpu.PrefetchScalarGridSpec(
            num_scalar_prefetch=2,
            grid=(R // bm,),
            in_specs=[pl.BlockSpec((bm, Dx), rows),
                      pl.BlockSpec((1, 1, D, H), wsel), pl.BlockSpec((1, 1, D, H), wsel),
                      pl.BlockSpec((1, 1, H, D), wsel)],
            out_specs=pl.BlockSpec((bm, D // 2), rows),
        ),
        out_shape=jax.ShapeDtypeStruct((R, Dx), jnp.uint32),
        input_output_aliases={2: 0},
        compiler_params=pltpu.CompilerParams(dimension_semantics=("arbitrary",), vmem_limit_bytes=VMEM_LIMIT),
        name="experts",
    )(blk_e, nused, xg, w1, w3, w2)


def _combine_kernel(alpha, cnt_ref, lstart_ref, goff_ref, tot_ref, x_ref, shift_ref, scale_ref, gate_ref,
                    p_ref, yg_ref, s1_ref, s3_ref, s2_ref, lg_ref, lb_ref, o_ref, ys_ref, sem):
    j = pl.program_id(0)
    cur = j % 2

    @pl.when(j == 0)
    def _():
        ys_ref[...] = jnp.zeros_like(ys_ref)
        _segment_copies(cnt_ref, lstart_ref, goff_ref, j, ys_ref.at[0], yg_ref, sem.at[0], False)

    @pl.when(j + 1 < pl.num_programs(0))
    def _():
        _segment_copies(cnt_ref, lstart_ref, goff_ref, j + 1, ys_ref.at[1 - cur], yg_ref, sem.at[1 - cur], False)

    x = x_ref[...]
    h2 = (x * (1.0 + scale_ref[0]) + shift_ref[0]).astype(_BF16)
    shared_hid = jnp.dot(h2, s1_ref[...], preferred_element_type=_F32)
    shared_hid = shared_hid * jax.nn.sigmoid(shared_hid) * jnp.dot(h2, s3_ref[...], preferred_element_type=_F32)
    shared = jnp.dot(shared_hid.astype(_BF16), s2_ref[...], preferred_element_type=_F32)

    _wait_rows(tot_ref, j, ys_ref.at[cur], yg_ref, sem.at[cur], False)
    onehot = p_ref[0]
    lo, hi = _unpack_halves(ys_ref[cur])
    sum_rows = (((0,), (0,)), ((), ()))
    routed = jnp.concatenate([lax.dot_general(onehot, lo, sum_rows, preferred_element_type=_F32),
                              lax.dot_general(onehot, hi, sum_rows, preferred_element_type=_F32)], axis=1)
    o_ref[...] = _layer_norm(alpha * x + gate_ref[0] * (routed + shared), lg_ref[...], lb_ref[...])


def _combine(alpha, tables, x1, shift, scale, gate, onehot, yg, s1, s3, s2, lg, lb, tiles_per_seq):
    T, D = x1.shape
    tm = TOKEN_TILE
    nt = T // tm
    H = s1.shape[1]
    vec = pl.BlockSpec((1, 1, D), lambda j, *_: (j // tiles_per_seq, 0, 0))
    const = lambda shape: pl.BlockSpec(shape, lambda j, *_: (0,) * len(shape))
    return pl.pallas_call(
        functools.partial(_combine_kernel, alpha),
        grid_spec=pltpu.PrefetchScalarGridSpec(
            num_scalar_prefetch=4,
            grid=(nt,),
            in_specs=[pl.BlockSpec((tm, D), lambda j, *_: (j, 0)), vec, vec, vec,
                      pl.BlockSpec((1, SLOT_ROWS, tm), lambda j, *_: (j, 0, 0)),
                      pl.BlockSpec(memory_space=pl.ANY),
                      const((D, H)), const((D, H)), const((H, D)), const((1, D)), const((1, D))],
            out_specs=pl.BlockSpec((tm, D), lambda j, *_: (j, 0)),
            scratch_shapes=[pltpu.VMEM((2, SLOT_ROWS, D // 2), jnp.uint32), pltpu.SemaphoreType.DMA((2,))],
        ),
        out_shape=jax.ShapeDtypeStruct((T, D), _F32),
        compiler_params=pltpu.CompilerParams(dimension_semantics=("arbitrary",), vmem_limit_bytes=VMEM_LIMIT),
        name="combine",
    )(*tables, x1, shift, scale, gate, onehot, yg, s1, s3, s2, lg.reshape(1, D), lb.reshape(1, D))


def _pack_heads(w):
    H, d, _ = w.shape
    eye = jnp.eye(HEAD_PACK, dtype=w.dtype)
    packed = jnp.einsum("kiab,ij->kiajb", w.reshape(H // HEAD_PACK, HEAD_PACK, d, d), eye)
    return packed.reshape(H // HEAD_PACK, HEAD_PACK * d, HEAD_PACK * d)


def _round_up(v, m):
    return (v + m - 1) // m * m


def _dispatch_tables(cnt_f, n_blocks):
    cnt_true = cnt_f.astype(jnp.int32)
    cnt = _round_up(cnt_true, SUBLANES)
    lstart = jnp.cumsum(cnt, axis=1) - cnt
    seg = jnp.stack([lstart, lstart + cnt_true], axis=1).astype(_F32)
    seg = jnp.concatenate([seg, seg], axis=2)
    per_row = jnp.sum(cnt, axis=0)
    region = _round_up(per_row, EXPERT_BLOCK)
    region_end = jnp.cumsum(region)
    goff = (region_end - region)[None, :] + jnp.cumsum(cnt, axis=0) - cnt
    tot = jnp.sum(cnt, axis=1)
    nused = (region_end[-1] // EXPERT_BLOCK).reshape(1)
    blk_first = jnp.arange(n_blocks, dtype=jnp.int32) * EXPERT_BLOCK
    blk_row = jnp.sum((region_end[None, :] <= blk_first[:, None]).astype(jnp.int32), axis=1)
    blk_row = jnp.minimum(blk_row, N_EXPERTS - 1)
    blk_e = _expert_of_row(blk_row)
    tables = (cnt.reshape(-1), lstart.reshape(-1).astype(jnp.int32), goff.reshape(-1).astype(jnp.int32),
              tot.astype(jnp.int32))
    return tables, seg, blk_e.astype(jnp.int32), nused.astype(jnp.int32)


def kernel(x, c, ada_w, ada_b, w_in, sgu_ln_g, sgu_ln_b, sgu_w, sgu_b, conv_w, conv_b, rg_wa, rg_ba, rg_wx, rg_bx, rg_lambda, w_branch_a, w_branch_b, w_out, ln1_g, ln1_b, router_w, router_b, exp_w1, exp_w3, exp_w2, sh_w1, sh_w3, sh_w2, ln2_g, ln2_b):
    B, S, D = x.shape
    L = ada_w.shape[0]
    T = B * S
    tm = TOKEN_TILE
    assert S % tm == 0 and S % MIXER_TILE == 0 and MIXER_TILE % SGU_CHUNK == 0 and D % 256 == 0
    assert router_w.shape[2] == N_EXPERTS
    alpha = float((2 * L) ** 0.25)
    tiles_per_seq = S // tm
    nt = T // tm
    n_blocks = (T * TOP_K + nt * N_EXPERTS * (SUBLANES - 1)) // EXPERT_BLOCK + N_EXPERTS + 1
    total_rows = n_blocks * EXPERT_BLOCK
    row_perm = _expert_of_row(jnp.arange(N_EXPERTS))

    xg = jnp.zeros((total_rows, D // 2 + ROW_WORDS_EXTRA), jnp.uint32)
    ada = _ada_all(c, ada_w, ada_b)
    bf = lambda a: a.astype(_BF16)
    for l in range(L):
        shift1, scale1, gate1, shift2, scale2, gate2 = [
            ada[l, :, k * D:(k + 1) * D].reshape(B, 1, D) for k in range(6)]
        x1 = _mixer(alpha, x, shift1, scale1, gate1, bf(w_in[l]), sgu_ln_g[l], sgu_ln_b[l], sgu_w[l],
                    sgu_b[l].T, conv_w[l], conv_b[l], bf(_pack_heads(rg_wa[l])), rg_ba[l], bf(_pack_heads(rg_wx[l])), rg_bx[l],
                    rg_lambda[l], bf(w_branch_a[l]), bf(w_branch_b[l]), bf(w_out[l]), ln1_g[l], ln1_b[l])
        x1 = x1.reshape(T, D)
        wr_rows = router_w[l].T[row_perm]
        br_rows = router_b[l][row_perm].reshape(N_EXPERTS, 1)
        digits, wn, cnt = _router(x1, shift2, scale2, wr_rows, br_rows, tiles_per_seq)
        tables, seg, blk_e, nused = _dispatch_tables(cnt[:, 0, :], n_blocks)
        wn_t = jnp.pad(wn.transpose(0, 2, 1), ((0, 0), (0, 0), (0, N_EXPERTS)))
        xg, onehot = _dispatch(tables, x1, shift2, scale2, digits, wn_t, seg, xg, tiles_per_seq)
        xg = _experts(l, blk_e, nused, xg, exp_w1, exp_w3, exp_w2)
        x2 = _combine(alpha, tables, x1, shift2, scale2, gate2, onehot,
                      xg, bf(sh_w1[l]), bf(sh_w3[l]), bf(sh_w2[l]), ln2_g[l], ln2_b[l], tiles_per_seq)
        x = x2.reshape(B, S, D)
    return x
```

```python
import functools

import jax
import jax.numpy as jnp
from jax import lax
from jax.experimental import pallas as pl
from jax.experimental.pallas import tpu as pltpu

SGU_CHUNK = 128
SGU_GROUPS = 8
RG_HEADS = 8
CONV_WIDTH = 4
RG_C = 8.0
N_EXPERTS = 64
TOP_K = 8
N_GROUPS = 8
GROUP_SIZE = N_EXPERTS // N_GROUPS
TOPK_GROUPS = 4
ROUTED_SCALE = 2.5
LN_EPS = 1e-5

SUBLANES = 8
TOKEN_TILE = 256
MIXER_TILE = 512
ROUTER_TILES = 4
SLOT_ROWS = TOP_K * TOKEN_TILE + N_EXPERTS * SUBLANES
EXPERT_BLOCK = 1024
SLOT_CHUNK = 1280
SLOT_RADIX = 64.0
SLOT_NONE = 255.0
OWNER_LANES = 2 * N_EXPERTS
ROW_WORDS_EXTRA = 128
HEAD_PACK = 2
VMEM_LIMIT = 56 * 1024 * 1024

_F32 = jnp.float32
_BF16 = jnp.bfloat16


_GELU_K1 = 2.0 * (2.0 / 3.141592653589793) ** 0.5
_GELU_K2 = _GELU_K1 * 0.044715


def _sigmoid(x):
    return 0.5 * jnp.tanh(0.5 * x) + 0.5


def _gelu(x):
    half = 0.5 * x
    return half + half * jnp.tanh(x * (0.5 * _GELU_K1 + 0.5 * _GELU_K2 * (x * x)))


def _layer_norm(x, g, b):
    mu = jnp.mean(x, axis=-1, keepdims=True)
    xc = x - mu
    var = jnp.mean(xc * xc, axis=-1, keepdims=True)
    return xc * lax.rsqrt(var + LN_EPS) * g + b


def _pack_halves(v):
    n = v.shape[1] // 2
    lo = lax.bitcast_convert_type(v[:, :n], jnp.uint32)
    hi = lax.bitcast_convert_type(v[:, n:], jnp.uint32)
    return (hi & jnp.uint32(0xFFFF0000)) | (lo >> 16)


def _unpack_halves(u):
    lo = lax.bitcast_convert_type(u << 16, _F32).astype(_BF16)
    hi = lax.bitcast_convert_type(u & jnp.uint32(0xFFFF0000), _F32).astype(_BF16)
    return lo, hi


def _ada_kernel(c_ref, w_ref, b_ref, o_ref):
    c = c_ref[...]
    c_act = c * jax.nn.sigmoid(c)
    o_ref[0] = jnp.dot(c_act, w_ref[0], preferred_element_type=_F32,
                       precision=lax.Precision.HIGHEST) + b_ref[0]


def _ada_all(c, ada_w, ada_b):
    L, D, W = ada_w.shape
    B = c.shape[0]
    nb = W // D
    return pl.pallas_call(
        _ada_kernel,
        grid=(L, nb),
        in_specs=[pl.BlockSpec((B, D), lambda l, n: (0, 0)),
                  pl.BlockSpec((1, D, D), lambda l, n: (l, 0, n)),
                  pl.BlockSpec((1, 1, D), lambda l, n: (l, 0, n))],
        out_specs=pl.BlockSpec((1, B, D), lambda l, n: (l, 0, n)),
        out_shape=jax.ShapeDtypeStruct((L, B, W), _F32),
        name="ada",
    )(c, ada_w, ada_b.reshape(L, 1, W))


def _scan_rows(a, b, carry):
    tm, C = a.shape
    sub = lax.broadcasted_iota(jnp.int32, (SUBLANES, C), 0)
    steps = [(d, sub >= d) for d in (1, 2, 4)]
    tiles = []
    for k in range(tm // SUBLANES):
        rows = slice(k * SUBLANES, (k + 1) * SUBLANES)
        ak, bk = a[rows], b[rows]
        for d, keep in steps:
            a_sh = jnp.where(keep, pltpu.roll(ak, d, axis=0), 1.0)
            b_sh = jnp.where(keep, pltpu.roll(bk, d, axis=0), 0.0)
            bk = bk + ak * b_sh
            ak = ak * a_sh
        h = bk + ak * carry
        carry = h[SUBLANES - 1:SUBLANES]
        tiles.append(h)
    return jnp.concatenate(tiles, axis=0)


def _mixer_kernel(alpha, x_ref, shift_ref, scale_ref, gate_ref, win_ref, slg_ref, slb_ref, sw_ref, sbt_ref,
                  cw_ref, cb_ref, wa_ref, ba_ref, wx_ref, bx_ref, lam_ref, wba_ref, wbb_ref, wo_ref,
                  lg_ref, lb_ref, o_ref, rbuf, hprev, sp_ref):
    tm, D = x_ref.shape[1], x_ref.shape[2]
    C = SGU_CHUNK
    si = pl.program_id(1)

    @pl.when(si == 0)
    def _():
        rbuf[0:SUBLANES, :] = jnp.zeros((SUBLANES, D), _F32)
        hprev[...] = jnp.zeros_like(hprev)

    x = x_ref[0]
    h = (x * (1.0 + scale_ref[0]) + shift_ref[0]).astype(_BF16)

    def proj(k):
        return jnp.dot(h, win_ref[:, k * D:(k + 1) * D], preferred_element_type=_F32)

    vn = _layer_norm(_gelu(proj(1)), slg_ref[...], slb_ref[...]).astype(_BF16)
    tri = (lax.broadcasted_iota(jnp.int32, (C, C), 0) >= lax.broadcasted_iota(jnp.int32, (C, C), 1))
    for g in range(SGU_GROUPS):
        wm = jnp.where(tri, sw_ref[g], 0.0).astype(_BF16)
        bias = sbt_ref[:, g:g + 1]
        for cc in range(tm // C):
            blk = jnp.dot(wm, vn[cc * C:(cc + 1) * C, g * C:(g + 1) * C], preferred_element_type=_F32)
            sp_ref[cc * C:(cc + 1) * C, g * C:(g + 1) * C] = blk + bias
    y_a = (_gelu(proj(0)) * sp_ref[...]).astype(_BF16)

    rbuf[SUBLANES:SUBLANES + tm, :] = proj(3)
    r_in = cb_ref[...] + cw_ref[CONV_WIDTH - 1:CONV_WIDTH, :] * rbuf[SUBLANES:SUBLANES + tm, :]
    for dlt in range(1, CONV_WIDTH):
        r_in = r_in + cw_ref[CONV_WIDTH - 1 - dlt:CONV_WIDTH - dlt, :] * rbuf[SUBLANES - dlt:SUBLANES - dlt + tm, :]
    rbuf[0:SUBLANES, :] = rbuf[tm:tm + SUBLANES, :]
    r_bf = r_in.astype(_BF16)
    hd = wa_ref.shape[1]
    ra = jnp.concatenate([jnp.dot(r_bf[:, k * hd:(k + 1) * hd], wa_ref[k], preferred_element_type=_F32)
                          for k in range(D // hd)], axis=1)
    rx = jnp.concatenate([jnp.dot(r_bf[:, k * hd:(k + 1) * hd], wx_ref[k], preferred_element_type=_F32)
                          for k in range(D // hd)], axis=1)
    r_gate = _sigmoid(ra + ba_ref[...])
    i_gate = _sigmoid(rx + bx_ref[...])
    neg_lam = -lam_ref[...]
    softplus = jnp.maximum(neg_lam, 0.0) + jnp.log1p(jnp.exp(-jnp.abs(neg_lam)))
    log_a = (-RG_C) * r_gate * softplus
    a = jnp.exp(log_a)
    t = jnp.tanh(log_a)
    inp = jnp.sqrt(-2.0 * t / (1.0 - t)) * (i_gate * r_in)
    h_seq = _scan_rows(a, inp, hprev[...])
    hprev[...] = h_seq[tm - 1:tm, :]
    y_b = (_gelu(proj(2)) * h_seq).astype(_BF16)

    merged = (_sigmoid(proj(4)) * jnp.dot(y_a, wba_ref[...], preferred_element_type=_F32)
              + _sigmoid(proj(5)) * jnp.dot(y_b, wbb_ref[...], preferred_element_type=_F32))
    mix = jnp.dot(merged.astype(_BF16), wo_ref[...], preferred_element_type=_F32)
    o_ref[0] = _layer_norm(alpha * x + gate_ref[0] * mix, lg_ref[...], lb_ref[...])


def _const_spec(shape):
    nd = len(shape)
    return pl.BlockSpec(shape, lambda b, s: (0,) * nd, pipeline_mode=pl.Buffered(1))


def _mixer(alpha, x, shift, scale, gate, w_in, slg, slb, sw, sbt, cw, cb, wa, ba, wx, bx, lam, wba, wbb, wo, lg, lb):
    B, S, D = x.shape
    tm = MIXER_TILE
    vec = pl.BlockSpec((1, 1, D), lambda b, s: (b, 0, 0))
    row = lambda a: a.reshape(1, D)
    consts = [w_in, row(slg), row(slb), sw, sbt, cw, row(cb), wa, row(ba), wx, row(bx), row(lam), wba, wbb, wo,
              row(lg), row(lb)]
    return pl.pallas_call(
        functools.partial(_mixer_kernel, alpha),
        grid=(B, S // tm),
        in_specs=[pl.BlockSpec((1, tm, D), lambda b, s: (b, s, 0)), vec, vec, vec]
                 + [_const_spec(a.shape) for a in consts],
        out_specs=pl.BlockSpec((1, tm, D), lambda b, s: (b, s, 0)),
        out_shape=jax.ShapeDtypeStruct((B, S, D), _F32),
        scratch_shapes=[pltpu.VMEM((tm + SUBLANES, D), _F32), pltpu.VMEM((1, D), _F32), pltpu.VMEM((tm, D), _F32)],
        compiler_params=pltpu.CompilerParams(dimension_semantics=("arbitrary", "arbitrary"),
                                             vmem_limit_bytes=VMEM_LIMIT),
        name="mixer",
    )(x, shift, scale, gate, *consts)


def _expert_of_row(p):
    return (p % N_GROUPS) * GROUP_SIZE + p // N_GROUPS


def _router_kernel(x_ref, shift_ref, scale_ref, wr_ref, br_ref, digits_ref, wn_ref, cnt_ref):
    tm = TOKEN_TILE
    G, J = N_GROUPS, GROUP_SIZE
    h2 = x_ref[...] * (1.0 + scale_ref[0]) + shift_ref[0]
    wr = wr_ref[...]
    w_hi, h_hi = wr.astype(_BF16), h2.astype(_BF16)
    w_lo, h_lo = (wr - w_hi.astype(_F32)).astype(_BF16), (h2 - h_hi.astype(_F32)).astype(_BF16)
    nt_dims = (((1,), (1,)), ((), ()))
    logits = (lax.dot_general(w_hi, h_hi, nt_dims, preferred_element_type=_F32)
              + lax.dot_general(w_hi, h_lo, nt_dims, preferred_element_type=_F32)
              + lax.dot_general(w_lo, h_hi, nt_dims, preferred_element_type=_F32))
    scores = jax.nn.sigmoid(logits)
    biased = scores + br_ref[...]
    slabs = [biased[j * G:(j + 1) * G, :] for j in range(J)]

    m1, m2 = slabs[0], jnp.full_like(slabs[0], -jnp.inf)
    for v in slabs[1:]:
        m2 = jnp.maximum(m2, jnp.minimum(m1, v))
        m1 = jnp.maximum(m1, v)
    gs = m1 + m2
    gidx = lax.broadcasted_iota(jnp.int32, gs.shape, 0)
    grank = jnp.zeros(gs.shape, jnp.int32)
    for g2 in range(G):
        other = gs[g2:g2 + 1, :]
        beats = (other > gs) | ((other == gs) & (g2 < gidx))
        grank = grank + beats.astype(jnp.int32)
    keep = grank < TOPK_GROUPS
    masked = [jnp.where(keep, v, -jnp.inf) for v in slabs]

    eidx = [gidx * J + j for j in range(J)]
    picked = [jnp.zeros(gs.shape, jnp.bool_) for _ in range(J)]
    for _ in range(TOP_K):
        m = masked[0]
        for j in range(1, J):
            m = jnp.maximum(m, masked[j])
        top = jnp.max(m, axis=0, keepdims=True)
        cand = jnp.where(masked[0] == top, eidx[0], N_EXPERTS)
        for j in range(1, J):
            cand = jnp.minimum(cand, jnp.where(masked[j] == top, eidx[j], N_EXPERTS))
        first = jnp.min(cand, axis=0, keepdims=True)
        for j in range(J):
            hit = eidx[j] == first
            picked[j] = picked[j] | hit
            masked[j] = jnp.where(hit, -jnp.inf, masked[j])
    sel_all = jnp.concatenate(picked, axis=0)
    w_sel = scores * sel_all.astype(_F32)
    wn_all = w_sel / jnp.sum(w_sel, axis=0, keepdims=True) * ROUTED_SCALE

    E = N_EXPERTS
    upper = (lax.broadcasted_iota(jnp.int32, (tm, tm), 0) < lax.broadcasted_iota(jnp.int32, (tm, tm), 1))
    lower = (lax.broadcasted_iota(jnp.int32, (E, E), 1) < lax.broadcasted_iota(jnp.int32, (E, E), 0))
    ones = jnp.ones((SUBLANES, tm), _BF16)
    for t in range(ROUTER_TILES):
        sel = sel_all[:, t * tm:(t + 1) * tm]
        sel_f = sel.astype(_F32)
        sel_b = sel_f.astype(_BF16)
        wn_ref[t] = wn_all[:, t * tm:(t + 1) * tm]
        tok_rank = jnp.dot(sel_b, upper.astype(_BF16), preferred_element_type=_F32)
        cnt = jnp.sum(sel_f, axis=1, keepdims=True)
        cnt_tiles = jnp.floor((cnt + (SUBLANES - 1)) * (1.0 / SUBLANES))
        start = jnp.dot(lower.astype(_BF16), jnp.broadcast_to(cnt_tiles, (E, 128)).astype(_BF16),
                        preferred_element_type=_F32)[:, 0:1] * float(SUBLANES)
        slot1 = start + tok_rank + 1.0
        hi = jnp.floor(slot1 * (1.0 / SLOT_RADIX))
        lo = slot1 - hi * SLOT_RADIX
        digits = jnp.concatenate([jnp.where(sel, hi, SLOT_NONE), jnp.where(sel, lo, 0.0)], axis=1)
        digits_ref[t] = jnp.concatenate([digits, jnp.zeros_like(digits)], axis=0).astype(_BF16)
        cnt_ref[t] = lax.dot_general(ones, sel_b, (((1,), (1,)), ((), ())), preferred_element_type=_F32)


def _router(x1, shift, scale, wr_rows, br_rows, tiles_per_seq):
    T, D = x1.shape
    tm = TOKEN_TILE
    nt = T // tm
    rt = ROUTER_TILES
    vec = pl.BlockSpec((1, 1, D), lambda j: (j * rt // tiles_per_seq, 0, 0))
    return pl.pallas_call(
        _router_kernel,
        grid=(nt // rt,),
        in_specs=[pl.BlockSpec((rt * tm, D), lambda j: (j, 0)), vec, vec,
                  pl.BlockSpec((N_EXPERTS, D), lambda j: (0, 0)),
                  pl.BlockSpec((N_EXPERTS, 1), lambda j: (0, 0))],
        out_specs=[pl.BlockSpec((rt, OWNER_LANES, 2 * tm), lambda j: (j, 0, 0)),
                   pl.BlockSpec((rt, N_EXPERTS, tm), lambda j: (j, 0, 0)),
                   pl.BlockSpec((rt, SUBLANES, N_EXPERTS), lambda j: (j, 0, 0))],
        out_shape=[jax.ShapeDtypeStruct((nt, OWNER_LANES, 2 * tm), _BF16),
                   jax.ShapeDtypeStruct((nt, N_EXPERTS, tm), _F32),
                   jax.ShapeDtypeStruct((nt, SUBLANES, N_EXPERTS), _F32)],
        compiler_params=pltpu.CompilerParams(dimension_semantics=("arbitrary",), vmem_limit_bytes=VMEM_LIMIT),
        name="router",
    )(x1, shift, scale, wr_rows, br_rows)


def _segment_copies(cnt_ref, lstart_ref, goff_ref, j, local_ref, global_ref, sem, to_global):
    for p in range(N_EXPERTS):
        n = pl.multiple_of(cnt_ref[j * N_EXPERTS + p], SUBLANES)
        ls = pl.multiple_of(lstart_ref[j * N_EXPERTS + p], SUBLANES)
        go = pl.multiple_of(goff_ref[j * N_EXPERTS + p], SUBLANES)

        @pl.when(n > 0)
        def _():
            loc = local_ref.at[pl.ds(ls, n)]
            glo = global_ref.at[pl.ds(go, n), pl.ds(0, local_ref.shape[1])]
            if to_global:
                pltpu.make_async_copy(loc, glo, sem).start()
            else:
                pltpu.make_async_copy(glo, loc, sem).start()


def _wait_rows(tot_ref, j, local_ref, global_ref, sem, to_global):
    n = pl.multiple_of(tot_ref[j], SUBLANES)

    @pl.when(n > 0)
    def _():
        loc = local_ref.at[pl.ds(0, n)]
        glo = global_ref.at[pl.ds(0, n), pl.ds(0, local_ref.shape[1])]
        if to_global:
            pltpu.make_async_copy(loc, glo, sem).wait()
        else:
            pltpu.make_async_copy(glo, loc, sem).wait()


def _dispatch_kernel(cnt_ref, lstart_ref, goff_ref, tot_ref, x_ref, shift_ref, scale_ref, digits_ref, wn_ref, seg_ref,
                     xg_prev_ref, xg_ref, p_ref, xs_ref, sem):
    del xg_prev_ref
    j = pl.program_id(0)
    tm, D = x_ref.shape
    E = N_EXPERTS
    cur = j % 2
    h2 = (x_ref[...] * (1.0 + scale_ref[0]) + shift_ref[0]).astype(_BF16)
    wn = wn_ref[0]
    w1 = wn.astype(_BF16)
    r1 = wn - w1.astype(_F32)
    w2 = r1.astype(_BF16)
    w3 = (r1 - w2.astype(_F32)).astype(_BF16)
    w12 = (w1.astype(_F32) + pltpu.roll(w2.astype(_F32), E, axis=1)).astype(_BF16)
    src = jnp.concatenate([h2, w12, w3], axis=1)
    digits = digits_ref[0]
    seg_lo, seg_hi = seg_ref[0, 0:1, :], seg_ref[0, 1:2, :]
    for c in range(SLOT_ROWS // SLOT_CHUNK):
        rows = slice(c * SLOT_CHUNK, (c + 1) * SLOT_CHUNK)
        base = float(c * SLOT_CHUNK)
        rows_e = lax.broadcasted_iota(jnp.int32, (SLOT_CHUNK, OWNER_LANES), 0).astype(_F32) + base
        owner = jnp.where((rows_e >= seg_lo) & (rows_e < seg_hi), 1.0, 0.0)
        dig = jnp.dot(owner.astype(_BF16), digits, preferred_element_type=_F32)
        rows_t = lax.broadcasted_iota(jnp.int32, (SLOT_CHUNK, tm), 0).astype(_F32) + (base + 1.0)
        onehot = jnp.where(dig[:, :tm] * SLOT_RADIX + dig[:, tm:] == rows_t, 1.0, 0.0).astype(_BF16)
        p_ref[0, rows, :] = onehot
        picked = jnp.dot(onehot, src, preferred_element_type=_F32)
        w_row = jnp.sum(owner * (picked[:, D:D + OWNER_LANES] + picked[:, D + OWNER_LANES:]), axis=1, keepdims=True)
        xs_ref[cur, rows, :D // 2] = _pack_halves(picked[:, :D])
        xs_ref[cur, rows, D // 2:] = lax.bitcast_convert_type(
            jnp.broadcast_to(w_row, (SLOT_CHUNK, ROW_WORDS_EXTRA)), jnp.uint32)
    _segment_copies(cnt_ref, lstart_ref, goff_ref, j, xs_ref.at[cur], xg_ref, sem.at[cur], True)

    @pl.when(j > 0)
    def _():
        _wait_rows(tot_ref, j - 1, xs_ref.at[1 - cur], xg_ref, sem.at[1 - cur], True)

    @pl.when(j == pl.num_programs(0) - 1)
    def _():
        _wait_rows(tot_ref, j, xs_ref.at[cur], xg_ref, sem.at[cur], True)


def _dispatch(tables, x1, shift, scale, digits, wn_t, seg_rows, xg_prev, tiles_per_seq):
    T, D = x1.shape
    tm = TOKEN_TILE
    nt = T // tm
    vec = pl.BlockSpec((1, 1, D), lambda j, *_: (j // tiles_per_seq, 0, 0))
    return pl.pallas_call(
        _dispatch_kernel,
        grid_spec=pltpu.PrefetchScalarGridSpec(
            num_scalar_prefetch=4,
            grid=(nt,),
            in_specs=[pl.BlockSpec((tm, D), lambda j, *_: (j, 0)), vec, vec,
                      pl.BlockSpec((1, OWNER_LANES, 2 * tm), lambda j, *_: (j, 0, 0)),
                      pl.BlockSpec((1, tm, OWNER_LANES), lambda j, *_: (j, 0, 0)),
                      pl.BlockSpec((1, 2, OWNER_LANES), lambda j, *_: (j, 0, 0)),
                      pl.BlockSpec(memory_space=pl.ANY)],
            out_specs=[pl.BlockSpec(memory_space=pl.ANY),
                       pl.BlockSpec((1, SLOT_ROWS, tm), lambda j, *_: (j, 0, 0))],
            scratch_shapes=[pltpu.VMEM((2, SLOT_ROWS, D // 2 + ROW_WORDS_EXTRA), jnp.uint32),
                            pltpu.SemaphoreType.DMA((2,))],
        ),
        out_shape=[jax.ShapeDtypeStruct(xg_prev.shape, jnp.uint32),
                   jax.ShapeDtypeStruct((nt, SLOT_ROWS, tm), _BF16)],
        input_output_aliases={10: 0},
        compiler_params=pltpu.CompilerParams(dimension_semantics=("arbitrary",), vmem_limit_bytes=VMEM_LIMIT),
        name="dispatch",
    )(*tables, x1, shift, scale, digits, wn_t, seg_rows, xg_prev)


def _expert_kernel(blk_ref, nused_ref, x_ref, w1_ref, w3_ref, w2_ref, o_ref):
    i = pl.program_id(0)

    @pl.when(i < nused_ref[0])
    def _():
        half = w1_ref.shape[2] // 2
        lo, hi = _unpack_halves(x_ref[:, :half])
        w_row = lax.bitcast_convert_type(x_ref[:, half:], _F32)
        w1 = w1_ref[0, 0].astype(_BF16)
        w3 = w3_ref[0, 0].astype(_BF16)
        a = (jnp.dot(lo, w1[:half], preferred_element_type=_F32) + jnp.dot(hi, w1[half:], preferred_element_type=_F32))
        b = (jnp.dot(lo, w3[:half], preferred_element_type=_F32) + jnp.dot(hi, w3[half:], preferred_element_type=_F32))
        hid = (a * jax.nn.sigmoid(a) * b).astype(_BF16)
        y = jnp.dot(hid, w2_ref[0, 0].astype(_BF16), preferred_element_type=_F32)
        y = y * jnp.tile(w_row, (1, y.shape[1] // w_row.shape[1]))
        o_ref[...] = _pack_halves(y.astype(_BF16).astype(_F32))


def _experts(layer, blk_e, nused, xg, w1, w3, w2):
    R, Dx = xg.shape
    bm = EXPERT_BLOCK
    _, _, D, H = w1.shape
    rows = lambda i, blk, nu: (jnp.minimum(i, nu[0] - 1), 0)
    wsel = lambda i, blk, nu: (layer, blk[i], 0, 0)
    return pl.pallas_call(
        _expert_kernel,
        grid_spec=pltpu.PrefetchScalarGridSpec(
            num_scalar_prefetch=2,
            grid=(R // bm,),
            in_specs=[pl.BlockSpec((bm, Dx), rows),
                      pl.BlockSpec((1, 1, D, H), wsel), pl.BlockSpec((1, 1, D, H), wsel),
                      pl.BlockSpec((1, 1, H, D), wsel)],
            out_specs=pl.BlockSpec((bm, D // 2), rows),
        ),
        out_shape=jax.ShapeDtypeStruct((R, Dx), jnp.uint32),
        input_output_aliases={2: 0},
        compiler_params=pltpu.CompilerParams(dimension_semantics=("arbitrary",), vmem_limit_bytes=VMEM_LIMIT),
        name="experts",
    )(blk_e, nused, xg, w1, w3, w2)


def _combine_kernel(alpha, cnt_ref, lstart_ref, goff_ref, tot_ref, x_ref, shift_ref, scale_ref, gate_ref,
                    p_ref, yg_ref, s1_ref, s3_ref, s2_ref, lg_ref, lb_ref, o_ref, ys_ref, sem):
    j = pl.program_id(0)
    cur = j % 2

    @pl.when(j == 0)
    def _():
        ys_ref[...] = jnp.zeros_like(ys_ref)
        _segment_copies(cnt_ref, lstart_ref, goff_ref, j, ys_ref.at[0], yg_ref, sem.at[0], False)

    @pl.when(j + 1 < pl.num_programs(0))
    def _():
        _segment_copies(cnt_ref, lstart_ref, goff_ref, j + 1, ys_ref.at[1 - cur], yg_ref, sem.at[1 - cur], False)

    x = x_ref[...]
    h2 = (x * (1.0 + scale_ref[0]) + shift_ref[0]).astype(_BF16)
    shared_hid = jnp.dot(h2, s1_ref[...], preferred_element_type=_F32)
    shared_hid = shared_hid * jax.nn.sigmoid(shared_hid) * jnp.dot(h2, s3_ref[...], preferred_element_type=_F32)
    shared = jnp.dot(shared_hid.astype(_BF16), s2_ref[...], preferred_element_type=_F32)

    _wait_rows(tot_ref, j, ys_ref.at[cur], yg_ref, sem.at[cur], False)
    onehot = p_ref[0]
    lo, hi = _unpack_halves(ys_ref[cur])
    sum_rows = (((0,), (0,)), ((), ()))
    routed = jnp.concatenate([lax.dot_general(onehot, lo, sum_rows, preferred_element_type=_F32),
                              lax.dot_general(onehot, hi, sum_rows, preferred_element_type=_F32)], axis=1)
    o_ref[...] = _layer_norm(alpha * x + gate_ref[0] * (routed + shared), lg_ref[...], lb_ref[...])


def _combine(alpha, tables, x1, shift, scale, gate, onehot, yg, s1, s3, s2, lg, lb, tiles_per_seq):
    T, D = x1.shape
    tm = TOKEN_TILE
    nt = T // tm
    H = s1.shape[1]
    vec = pl.BlockSpec((1, 1, D), lambda j, *_: (j // tiles_per_seq, 0, 0))
    const = lambda shape: pl.BlockSpec(shape, lambda j, *_: (0,) * len(shape))
    return pl.pallas_call(
        functools.partial(_combine_kernel, alpha),
        grid_spec=pltpu.PrefetchScalarGridSpec(
            num_scalar_prefetch=4,
            grid=(nt,),
            in_specs=[pl.BlockSpec((tm, D), lambda j, *_: (j, 0)), vec, vec, vec,
                      pl.BlockSpec((1, SLOT_ROWS, tm), lambda j, *_: (j, 0, 0)),
                      pl.BlockSpec(memory_space=pl.ANY),
                      const((D, H)), const((D, H)), const((H, D)), const((1, D)), const((1, D))],
            out_specs=pl.BlockSpec((tm, D), lambda j, *_: (j, 0)),
            scratch_shapes=[pltpu.VMEM((2, SLOT_ROWS, D // 2), jnp.uint32), pltpu.SemaphoreType.DMA((2,))],
        ),
        out_shape=jax.ShapeDtypeStruct((T, D), _F32),
        compiler_params=pltpu.CompilerParams(dimension_semantics=("arbitrary",), vmem_limit_bytes=VMEM_LIMIT),
        name="combine",
    )(*tables, x1, shift, scale, gate, onehot, yg, s1, s3, s2, lg.reshape(1, D), lb.reshape(1, D))


def _pack_heads(w):
    H, d, _ = w.shape
    eye = jnp.eye(HEAD_PACK, dtype=w.dtype)
    packed = jnp.einsum("kiab,ij->kiajb", w.reshape(H // HEAD_PACK, HEAD_PACK, d, d), eye)
    return packed.reshape(H // HEAD_PACK, HEAD_PACK * d, HEAD_PACK * d)


def _round_up(v, m):
    return (v + m - 1) // m * m


def _dispatch_tables(cnt_f, n_blocks):
    cnt_true = cnt_f.astype(jnp.int32)
    cnt = _round_up(cnt_true, SUBLANES)
    lstart = jnp.cumsum(cnt, axis=1) - cnt
    seg = jnp.stack([lstart, lstart + cnt_true], axis=1).astype(_F32)
    seg = jnp.concatenate([seg, seg], axis=2)
    per_row = jnp.sum(cnt, axis=0)
    region = _round_up(per_row, EXPERT_BLOCK)
    region_end = jnp.cumsum(region)
    goff = (region_end - region)[None, :] + jnp.cumsum(cnt, axis=0) - cnt
    tot = jnp.sum(cnt, axis=1)
    nused = (region_end[-1] // EXPERT_BLOCK).reshape(1)
    blk_first = jnp.arange(n_blocks, dtype=jnp.int32) * EXPERT_BLOCK
    blk_row = jnp.sum((region_end[None, :] <= blk_first[:, None]).astype(jnp.int32), axis=1)
    blk_row = jnp.minimum(blk_row, N_EXPERTS - 1)
    blk_e = _expert_of_row(blk_row)
    tables = (cnt.reshape(-1), lstart.reshape(-1).astype(jnp.int32), goff.reshape(-1).astype(jnp.int32),
              tot.astype(jnp.int32))
    return tables, seg, blk_e.astype(jnp.int32), nused.astype(jnp.int32)


def kernel(x, c, ada_w, ada_b, w_in, sgu_ln_g, sgu_ln_b, sgu_w, sgu_b, conv_w, conv_b, rg_wa, rg_ba, rg_wx, rg_bx, rg_lambda, w_branch_a, w_branch_b, w_out, ln1_g, ln1_b, router_w, router_b, exp_w1, exp_w3, exp_w2, sh_w1, sh_w3, sh_w2, ln2_g, ln2_b):
    B, S, D = x.shape
    L = ada_w.shape[0]
    T = B * S
    tm = TOKEN_TILE
    assert S % (ROUTER_TILES * tm) == 0 and S % MIXER_TILE == 0 and MIXER_TILE % SGU_CHUNK == 0 and D % 256 == 0
    assert router_w.shape[2] == N_EXPERTS
    alpha = float((2 * L) ** 0.25)
    tiles_per_seq = S // tm
    nt = T // tm
    n_blocks = (T * TOP_K + nt * N_EXPERTS * (SUBLANES - 1)) // EXPERT_BLOCK + N_EXPERTS + 1
    total_rows = n_blocks * EXPERT_BLOCK
    row_perm = _expert_of_row(jnp.arange(N_EXPERTS))

    xg = jnp.zeros((total_rows, D // 2 + ROW_WORDS_EXTRA), jnp.uint32)
    ada = _ada_all(c, ada_w, ada_b)
    bf = lambda a: a.astype(_BF16)
    for l in range(L):
        shift1, scale1, gate1, shift2, scale2, gate2 = [
            ada[l, :, k * D:(k + 1) * D].reshape(B, 1, D) for k in range(6)]
        x1 = _mixer(alpha, x, shift1, scale1, gate1, bf(w_in[l]), sgu_ln_g[l], sgu_ln_b[l], sgu_w[l],
                    sgu_b[l].T, conv_w[l], conv_b[l], bf(_pack_heads(rg_wa[l])), rg_ba[l], bf(_pack_heads(rg_wx[l])), rg_bx[l],
                    rg_lambda[l], bf(w_branch_a[l]), bf(w_branch_b[l]), bf(w_out[l]), ln1_g[l], ln1_b[l])
        x1 = x1.reshape(T, D)
        wr_rows = router_w[l].T[row_perm]
        br_rows = router_b[l][row_perm].reshape(N_EXPERTS, 1)
        digits, wn, cnt = _router(x1, shift2, scale2, wr_rows, br_rows, tiles_per_seq)
        tables, seg, blk_e, nused = _dispatch_tables(cnt[:, 0, :], n_blocks)
        wn_t = jnp.pad(wn.transpose(0, 2, 1), ((0, 0), (0, 0), (0, N_EXPERTS)))
        xg, onehot = _dispatch(tables, x1, shift2, scale2, digits, wn_t, seg, xg, tiles_per_seq)
        xg = _experts(l, blk_e, nused, xg, exp_w1, exp_w3, exp_w2)
        x2 = _combine(alpha, tables, x1, shift2, scale2, gate2, onehot,
                      xg, bf(sh_w1[l]), bf(sh_w3[l]), bf(sh_w2[l]), ln2_g[l], ln2_b[l], tiles_per_seq)
        x = x2.reshape(B, S, D)
    return x
```

```python
import functools

import jax
import jax.numpy as jnp
from jax import lax
from jax.experimental import pallas as pl
from jax.experimental.pallas import tpu as pltpu

SGU_CHUNK = 128
SGU_GROUPS = 8
RG_HEADS = 8
CONV_WIDTH = 4
RG_C = 8.0
N_EXPERTS = 64
TOP_K = 8
N_GROUPS = 8
GROUP_SIZE = N_EXPERTS // N_GROUPS
TOPK_GROUPS = 4
ROUTED_SCALE = 2.5
LN_EPS = 1e-5

SUBLANES = 8
TOKEN_TILE = 256
MIXER_TILE = 512
ROUTER_TILES = 4
SLOT_ROWS = TOP_K * TOKEN_TILE + N_EXPERTS * SUBLANES
EXPERT_BLOCK = 1024
EXPERT_PAIR = 2
SLOT_CHUNK = 1280
SLOT_TAIL = 256
SLOT_RADIX = 64.0
SLOT_NONE = 255.0
OWNER_LANES = 2 * N_EXPERTS
ROW_WORDS_EXTRA = 128
HEAD_PACK = 2
VMEM_LIMIT = 56 * 1024 * 1024

_F32 = jnp.float32
_BF16 = jnp.bfloat16


_GELU_K1 = 2.0 * (2.0 / 3.141592653589793) ** 0.5
_GELU_K2 = _GELU_K1 * 0.044715


def _sigmoid(x):
    return 0.5 * jnp.tanh(0.5 * x) + 0.5


def _gelu(x):
    half = 0.5 * x
    return half + half * jnp.tanh(x * (0.5 * _GELU_K1 + 0.5 * _GELU_K2 * (x * x)))


def _layer_norm(x, g, b):
    mu = jnp.mean(x, axis=-1, keepdims=True)
    xc = x - mu
    var = jnp.mean(xc * xc, axis=-1, keepdims=True)
    return xc * lax.rsqrt(var + LN_EPS) * g + b


def _pack_halves(v):
    n = v.shape[1] // 2
    lo = lax.bitcast_convert_type(v[:, :n], jnp.uint32)
    hi = lax.bitcast_convert_type(v[:, n:], jnp.uint32)
    return (hi & jnp.uint32(0xFFFF0000)) | (lo >> 16)


def _unpack_halves(u):
    lo = lax.bitcast_convert_type(u << 16, _F32).astype(_BF16)
    hi = lax.bitcast_convert_type(u & jnp.uint32(0xFFFF0000), _F32).astype(_BF16)
    return lo, hi


def _ada_kernel(c_ref, w_ref, b_ref, o_ref):
    c = c_ref[...]
    c_act = c * jax.nn.sigmoid(c)
    o_ref[0] = jnp.dot(c_act, w_ref[0], preferred_element_type=_F32,
                       precision=lax.Precision.HIGHEST) + b_ref[0]


def _ada_all(c, ada_w, ada_b):
    L, D, W = ada_w.shape
    B = c.shape[0]
    nb = W // D
    return pl.pallas_call(
        _ada_kernel,
        grid=(L, nb),
        in_specs=[pl.BlockSpec((B, D), lambda l, n: (0, 0)),
                  pl.BlockSpec((1, D, D), lambda l, n: (l, 0, n)),
                  pl.BlockSpec((1, 1, D), lambda l, n: (l, 0, n))],
        out_specs=pl.BlockSpec((1, B, D), lambda l, n: (l, 0, n)),
        out_shape=jax.ShapeDtypeStruct((L, B, W), _F32),
        name="ada",
    )(c, ada_w, ada_b.reshape(L, 1, W))


def _scan_rows(a, b, carry):
    tm, C = a.shape
    sub = lax.broadcasted_iota(jnp.int32, (SUBLANES, C), 0)
    steps = [(d, sub >= d) for d in (1, 2, 4)]
    tiles = []
    for k in range(tm // SUBLANES):
        rows = slice(k * SUBLANES, (k + 1) * SUBLANES)
        ak, bk = a[rows], b[rows]
        for d, keep in steps:
            a_sh = jnp.where(keep, pltpu.roll(ak, d, axis=0), 1.0)
            b_sh = jnp.where(keep, pltpu.roll(bk, d, axis=0), 0.0)
            bk = bk + ak * b_sh
            ak = ak * a_sh
        h = bk + ak * carry
        carry = h[SUBLANES - 1:SUBLANES]
        tiles.append(h)
    return jnp.concatenate(tiles, axis=0)


def _mixer_kernel(alpha, x_ref, shift_ref, scale_ref, gate_ref, win_ref, slg_ref, slb_ref, sw_ref, sbt_ref,
                  cw_ref, cb_ref, wa_ref, ba_ref, wx_ref, bx_ref, lam_ref, wba_ref, wbb_ref, wo_ref,
                  lg_ref, lb_ref, o_ref, rbuf, hprev, sp_ref):
    tm, D = x_ref.shape[1], x_ref.shape[2]
    C = SGU_CHUNK
    si = pl.program_id(1)

    @pl.when(si == 0)
    def _():
        rbuf[0:SUBLANES, :] = jnp.zeros((SUBLANES, D), _F32)
        hprev[...] = jnp.zeros_like(hprev)

    x = x_ref[0]
    h = (x * (1.0 + scale_ref[0]) + shift_ref[0]).astype(_BF16)

    def proj(k):
        return jnp.dot(h, win_ref[:, k * D:(k + 1) * D], preferred_element_type=_F32)

    vn = _layer_norm(_gelu(proj(1)), slg_ref[...], slb_ref[...]).astype(_BF16)
    tri = (lax.broadcasted_iota(jnp.int32, (C, C), 0) >= lax.broadcasted_iota(jnp.int32, (C, C), 1))
    for g in range(SGU_GROUPS):
        wm = jnp.where(tri, sw_ref[g], 0.0).astype(_BF16)
        bias = sbt_ref[:, g:g + 1]
        for cc in range(tm // C):
            blk = jnp.dot(wm, vn[cc * C:(cc + 1) * C, g * C:(g + 1) * C], preferred_element_type=_F32)
            sp_ref[cc * C:(cc + 1) * C, g * C:(g + 1) * C] = blk + bias
    y_a = (_gelu(proj(0)) * sp_ref[...]).astype(_BF16)

    rbuf[SUBLANES:SUBLANES + tm, :] = proj(3)
    r_in = cb_ref[...] + cw_ref[CONV_WIDTH - 1:CONV_WIDTH, :] * rbuf[SUBLANES:SUBLANES + tm, :]
    for dlt in range(1, CONV_WIDTH):
        r_in = r_in + cw_ref[CONV_WIDTH - 1 - dlt:CONV_WIDTH - dlt, :] * rbuf[SUBLANES - dlt:SUBLANES - dlt + tm, :]
    rbuf[0:SUBLANES, :] = rbuf[tm:tm + SUBLANES, :]
    r_bf = r_in.astype(_BF16)
    hd = wa_ref.shape[1]
    ra = jnp.concatenate([jnp.dot(r_bf[:, k * hd:(k + 1) * hd], wa_ref[k], preferred_element_type=_F32)
                          for k in range(D // hd)], axis=1)
    rx = jnp.concatenate([jnp.dot(r_bf[:, k * hd:(k + 1) * hd], wx_ref[k], preferred_element_type=_F32)
                          for k in range(D // hd)], axis=1)
    r_gate = _sigmoid(ra + ba_ref[...])
    i_gate = _sigmoid(rx + bx_ref[...])
    neg_lam = -lam_ref[...]
    softplus = jnp.maximum(neg_lam, 0.0) + jnp.log1p(jnp.exp(-jnp.abs(neg_lam)))
    log_a = (-RG_C) * r_gate * softplus
    a = jnp.exp(log_a)
    t = jnp.tanh(log_a)
    inp = jnp.sqrt(-2.0 * t / (1.0 - t)) * (i_gate * r_in)
    h_seq = _scan_rows(a, inp, hprev[...])
    hprev[...] = h_seq[tm - 1:tm, :]
    y_b = (_gelu(proj(2)) * h_seq).astype(_BF16)

    merged = (_sigmoid(proj(4)) * jnp.dot(y_a, wba_ref[...], preferred_element_type=_F32)
              + _sigmoid(proj(5)) * jnp.dot(y_b, wbb_ref[...], preferred_element_type=_F32))
    mix = jnp.dot(merged.astype(_BF16), wo_ref[...], preferred_element_type=_F32)
    o_ref[0] = _layer_norm(alpha * x + gate_ref[0] * mix, lg_ref[...], lb_ref[...])


def _const_spec(shape):
    nd = len(shape)
    return pl.BlockSpec(shape, lambda b, s: (0,) * nd, pipeline_mode=pl.Buffered(1))


def _mixer(alpha, x, shift, scale, gate, w_in, slg, slb, sw, sbt, cw, cb, wa, ba, wx, bx, lam, wba, wbb, wo, lg, lb):
    B, S, D = x.shape
    tm = MIXER_TILE
    vec = pl.BlockSpec((1, 1, D), lambda b, s: (b, 0, 0))
    row = lambda a: a.reshape(1, D)
    consts = [w_in, row(slg), row(slb), sw, sbt, cw, row(cb), wa, row(ba), wx, row(bx), row(lam), wba, wbb, wo,
              row(lg), row(lb)]
    return pl.pallas_call(
        functools.partial(_mixer_kernel, alpha),
        grid=(B, S // tm),
        in_specs=[pl.BlockSpec((1, tm, D), lambda b, s: (b, s, 0)), vec, vec, vec]
                 + [_const_spec(a.shape) for a in consts],
        out_specs=pl.BlockSpec((1, tm, D), lambda b, s: (b, s, 0)),
        out_shape=jax.ShapeDtypeStruct((B, S, D), _F32),
        scratch_shapes=[pltpu.VMEM((tm + SUBLANES, D), _F32), pltpu.VMEM((1, D), _F32), pltpu.VMEM((tm, D), _F32)],
        compiler_params=pltpu.CompilerParams(dimension_semantics=("arbitrary", "arbitrary"),
                                             vmem_limit_bytes=VMEM_LIMIT),
        name="mixer",
    )(x, shift, scale, gate, *consts)


def _expert_of_row(p):
    return (p % N_GROUPS) * GROUP_SIZE + p // N_GROUPS


def _router_kernel(x_ref, shift_ref, scale_ref, wr_ref, br_ref, digits_ref, wn_ref, cnt_ref):
    tm = TOKEN_TILE
    G, J = N_GROUPS, GROUP_SIZE
    h2 = x_ref[...] * (1.0 + scale_ref[0]) + shift_ref[0]
    wr = wr_ref[...]
    w_hi, h_hi = wr.astype(_BF16), h2.astype(_BF16)
    w_lo, h_lo = (wr - w_hi.astype(_F32)).astype(_BF16), (h2 - h_hi.astype(_F32)).astype(_BF16)
    nt_dims = (((1,), (1,)), ((), ()))
    logits = (lax.dot_general(w_hi, h_hi, nt_dims, preferred_element_type=_F32)
              + lax.dot_general(w_hi, h_lo, nt_dims, preferred_element_type=_F32)
              + lax.dot_general(w_lo, h_hi, nt_dims, preferred_element_type=_F32))
    scores = jax.nn.sigmoid(logits)
    biased = scores + br_ref[...]
    slabs = [biased[j * G:(j + 1) * G, :] for j in range(J)]

    m1, m2 = slabs[0], jnp.full_like(slabs[0], -jnp.inf)
    for v in slabs[1:]:
        m2 = jnp.maximum(m2, jnp.minimum(m1, v))
        m1 = jnp.maximum(m1, v)
    gs = m1 + m2
    gidx = lax.broadcasted_iota(jnp.int32, gs.shape, 0)
    grank = jnp.zeros(gs.shape, jnp.int32)
    for g2 in range(G):
        other = gs[g2:g2 + 1, :]
        beats = (other > gs) | ((other == gs) & (g2 < gidx))
        grank = grank + beats.astype(jnp.int32)
    keep = grank < TOPK_GROUPS
    masked = [jnp.where(keep, v, -jnp.inf) for v in slabs]

    eidx = [gidx * J + j for j in range(J)]
    picked = [jnp.zeros(gs.shape, jnp.bool_) for _ in range(J)]
    for _ in range(TOP_K):
        m = masked[0]
        for j in range(1, J):
            m = jnp.maximum(m, masked[j])
        top = jnp.max(m, axis=0, keepdims=True)
        cand = jnp.where(masked[0] == top, eidx[0], N_EXPERTS)
        for j in range(1, J):
            cand = jnp.minimum(cand, jnp.where(masked[j] == top, eidx[j], N_EXPERTS))
        first = jnp.min(cand, axis=0, keepdims=True)
        for j in range(J):
            hit = eidx[j] == first
            picked[j] = picked[j] | hit
            masked[j] = jnp.where(hit, -jnp.inf, masked[j])
    sel_all = jnp.concatenate(picked, axis=0)
    w_sel = scores * sel_all.astype(_F32)
    wn_all = w_sel / jnp.sum(w_sel, axis=0, keepdims=True) * ROUTED_SCALE

    E = N_EXPERTS
    upper = (lax.broadcasted_iota(jnp.int32, (tm, tm), 0) < lax.broadcasted_iota(jnp.int32, (tm, tm), 1))
    lower = (lax.broadcasted_iota(jnp.int32, (E, E), 1) < lax.broadcasted_iota(jnp.int32, (E, E), 0))
    ones = jnp.ones((SUBLANES, tm), _BF16)
    for t in range(ROUTER_TILES):
        sel = sel_all[:, t * tm:(t + 1) * tm]
        sel_f = sel.astype(_F32)
        sel_b = sel_f.astype(_BF16)
        wn_ref[t] = wn_all[:, t * tm:(t + 1) * tm]
        tok_rank = jnp.dot(sel_b, upper.astype(_BF16), preferred_element_type=_F32)
        cnt = jnp.sum(sel_f, axis=1, keepdims=True)
        cnt_tiles = jnp.floor((cnt + (SUBLANES - 1)) * (1.0 / SUBLANES))
        start = jnp.dot(lower.astype(_BF16), jnp.broadcast_to(cnt_tiles, (E, 128)).astype(_BF16),
                        preferred_element_type=_F32)[:, 0:1] * float(SUBLANES)
        slot1 = start + tok_rank + 1.0
        hi = jnp.floor(slot1 * (1.0 / SLOT_RADIX))
        lo = slot1 - hi * SLOT_RADIX
        digits = jnp.concatenate([jnp.where(sel, hi, SLOT_NONE), jnp.where(sel, lo, 0.0)], axis=1)
        digits_ref[t] = jnp.concatenate([digits, jnp.zeros_like(digits)], axis=0).astype(_BF16)
        cnt_ref[t] = lax.dot_general(ones, sel_b, (((1,), (1,)), ((), ())), preferred_element_type=_F32)


def _router(x1, shift, scale, wr_rows, br_rows, tiles_per_seq):
    T, D = x1.shape
    tm = TOKEN_TILE
    nt = T // tm
    rt = ROUTER_TILES
    vec = pl.BlockSpec((1, 1, D), lambda j: (j * rt // tiles_per_seq, 0, 0))
    return pl.pallas_call(
        _router_kernel,
        grid=(nt // rt,),
        in_specs=[pl.BlockSpec((rt * tm, D), lambda j: (j, 0)), vec, vec,
                  pl.BlockSpec((N_EXPERTS, D), lambda j: (0, 0)),
                  pl.BlockSpec((N_EXPERTS, 1), lambda j: (0, 0))],
        out_specs=[pl.BlockSpec((rt, OWNER_LANES, 2 * tm), lambda j: (j, 0, 0)),
                   pl.BlockSpec((rt, N_EXPERTS, tm), lambda j: (j, 0, 0)),
                   pl.BlockSpec((rt, SUBLANES, N_EXPERTS), lambda j: (j, 0, 0))],
        out_shape=[jax.ShapeDtypeStruct((nt, OWNER_LANES, 2 * tm), _BF16),
                   jax.ShapeDtypeStruct((nt, N_EXPERTS, tm), _F32),
                   jax.ShapeDtypeStruct((nt, SUBLANES, N_EXPERTS), _F32)],
        compiler_params=pltpu.CompilerParams(dimension_semantics=("arbitrary",), vmem_limit_bytes=VMEM_LIMIT),
        name="router",
    )(x1, shift, scale, wr_rows, br_rows)


def _segment_copies(cnt_ref, lstart_ref, goff_ref, j, local_ref, global_ref, sem, to_global, enable=None):
    for p in range(N_EXPERTS):
        n = pl.multiple_of(cnt_ref[j * N_EXPERTS + p], SUBLANES)
        ls = pl.multiple_of(lstart_ref[j * N_EXPERTS + p], SUBLANES)
        go = pl.multiple_of(goff_ref[j * N_EXPERTS + p], SUBLANES)

        @pl.when((n > 0) if enable is None else jnp.logical_and(n > 0, enable))
        def _():
            loc = local_ref.at[pl.ds(ls, n)]
            glo = global_ref.at[pl.ds(go, n), pl.ds(0, local_ref.shape[1])]
            if to_global:
                pltpu.make_async_copy(loc, glo, sem).start()
            else:
                pltpu.make_async_copy(glo, loc, sem).start()


def _wait_rows(tot_ref, j, local_ref, global_ref, sem, to_global):
    n = pl.multiple_of(tot_ref[j], SUBLANES)

    @pl.when(n > 0)
    def _():
        loc = local_ref.at[pl.ds(0, n)]
        glo = global_ref.at[pl.ds(0, n), pl.ds(0, local_ref.shape[1])]
        if to_global:
            pltpu.make_async_copy(loc, glo, sem).wait()
        else:
            pltpu.make_async_copy(glo, loc, sem).wait()


def _dispatch_kernel(cnt_ref, lstart_ref, goff_ref, tot_ref, x_ref, shift_ref, scale_ref, digits_ref, wn_ref, seg_ref,
                     xg_prev_ref, xg_ref, p_ref, xs_ref, sem):
    del xg_prev_ref
    j = pl.program_id(0)
    tm, D = x_ref.shape
    E = N_EXPERTS
    cur = j % 2
    prev = jnp.maximum(j - 1, 0)
    _segment_copies(cnt_ref, lstart_ref, goff_ref, prev, xs_ref.at[1 - cur], xg_ref, sem.at[1 - cur], True,
                    enable=j > 0)
    h2 = (x_ref[...] * (1.0 + scale_ref[0]) + shift_ref[0]).astype(_BF16)
    wn = wn_ref[0]
    w1 = wn.astype(_BF16)
    r1 = wn - w1.astype(_F32)
    w2 = r1.astype(_BF16)
    w3 = (r1 - w2.astype(_F32)).astype(_BF16)
    w12 = (w1.astype(_F32) + pltpu.roll(w2.astype(_F32), E, axis=1)).astype(_BF16)
    src = jnp.concatenate([h2, w12, w3], axis=1)
    digits = digits_ref[0]
    seg_lo, seg_hi = seg_ref[0, 0:1, :], seg_ref[0, 1:2, :]
    def sort_rows(first, size):
        rows = slice(first, first + size)
        rows_e = lax.broadcasted_iota(jnp.int32, (size, OWNER_LANES), 0).astype(_F32) + float(first)
        owner = jnp.where((rows_e >= seg_lo) & (rows_e < seg_hi), 1.0, 0.0)
        dig = jnp.dot(owner.astype(_BF16), digits, preferred_element_type=_F32)
        rows_t = lax.broadcasted_iota(jnp.int32, (size, tm), 0).astype(_F32) + float(first + 1)
        onehot = jnp.where(dig[:, :tm] * SLOT_RADIX + dig[:, tm:] == rows_t, 1.0, 0.0).astype(_BF16)
        p_ref[0, rows, :] = onehot
        picked = jnp.dot(onehot, src, preferred_element_type=_F32)
        w_row = jnp.sum(owner * (picked[:, D:D + OWNER_LANES] + picked[:, D + OWNER_LANES:]), axis=1, keepdims=True)
        xs_ref[cur, rows, :D // 2] = _pack_halves(picked[:, :D])
        xs_ref[cur, rows, D // 2:] = lax.bitcast_convert_type(
            jnp.broadcast_to(w_row, (size, ROW_WORDS_EXTRA)), jnp.uint32)

    body_rows = SLOT_ROWS - SLOT_TAIL
    for first in range(0, body_rows, SLOT_CHUNK):
        sort_rows(first, min(SLOT_CHUNK, body_rows - first))
    tail_used = tot_ref[j] > body_rows

    @pl.when(tail_used)
    def _():
        sort_rows(body_rows, SLOT_TAIL)

    @pl.when(jnp.logical_not(tail_used))
    def _():
        p_ref[0, body_rows:, :] = jnp.zeros((SLOT_TAIL, tm), _BF16)

    @pl.when(j > 0)
    def _():
        _wait_rows(tot_ref, j - 1, xs_ref.at[1 - cur], xg_ref, sem.at[1 - cur], True)

    @pl.when(j == pl.num_programs(0) - 1)
    def _():
        _segment_copies(cnt_ref, lstart_ref, goff_ref, j, xs_ref.at[cur], xg_ref, sem.at[cur], True)
        _wait_rows(tot_ref, j, xs_ref.at[cur], xg_ref, sem.at[cur], True)


def _dispatch(tables, x1, shift, scale, digits, wn_t, seg_rows, xg_prev, tiles_per_seq):
    T, D = x1.shape
    tm = TOKEN_TILE
    nt = T // tm
    vec = pl.BlockSpec((1, 1, D), lambda j, *_: (j // tiles_per_seq, 0, 0))
    return pl.pallas_call(
        _dispatch_kernel,
        grid_spec=pltpu.PrefetchScalarGridSpec(
            num_scalar_prefetch=4,
            grid=(nt,),
            in_specs=[pl.BlockSpec((tm, D), lambda j, *_: (j, 0)), vec, vec,
                      pl.BlockSpec((1, OWNER_LANES, 2 * tm), lambda j, *_: (j, 0, 0)),
                      pl.BlockSpec((1, tm, OWNER_LANES), lambda j, *_: (j, 0, 0)),
                      pl.BlockSpec((1, 2, OWNER_LANES), lambda j, *_: (j, 0, 0)),
                      pl.BlockSpec(memory_space=pl.ANY)],
            out_specs=[pl.BlockSpec(memory_space=pl.ANY),
                       pl.BlockSpec((1, SLOT_ROWS, tm), lambda j, *_: (j, 0, 0))],
            scratch_shapes=[pltpu.VMEM((2, SLOT_ROWS, D // 2 + ROW_WORDS_EXTRA), jnp.uint32),
                            pltpu.SemaphoreType.DMA((2,))],
        ),
        out_shape=[jax.ShapeDtypeStruct(xg_prev.shape, jnp.uint32),
                   jax.ShapeDtypeStruct((nt, SLOT_ROWS, tm), _BF16)],
        input_output_aliases={10: 0},
        compiler_params=pltpu.CompilerParams(dimension_semantics=("arbitrary",), vmem_limit_bytes=VMEM_LIMIT),
        name="dispatch",
    )(*tables, x1, shift, scale, digits, wn_t, seg_rows, xg_prev)


def _expert_kernel(blk_ref, nused_ref, x_ref, *refs):
    o_ref = refs[-1]
    i = pl.program_id(0)
    bm = EXPERT_BLOCK
    for k in range(EXPERT_PAIR):
        w1_ref, w3_ref, w2_ref = refs[3 * k:3 * k + 3]
        rows = slice(k * bm, (k + 1) * bm)

        @pl.when(i * EXPERT_PAIR + k < nused_ref[0])
        def _():
            half = w1_ref.shape[2] // 2
            lo, hi = _unpack_halves(x_ref[rows, :half])
            w_row = lax.bitcast_convert_type(x_ref[rows, half:], _F32)
            w1 = w1_ref[0, 0].astype(_BF16)
            w3 = w3_ref[0, 0].astype(_BF16)
            a = (jnp.dot(lo, w1[:half], preferred_element_type=_F32)
                 + jnp.dot(hi, w1[half:], preferred_element_type=_F32))
            b = (jnp.dot(lo, w3[:half], preferred_element_type=_F32)
                 + jnp.dot(hi, w3[half:], preferred_element_type=_F32))
            hid = (a * jax.nn.sigmoid(a) * b).astype(_BF16)
            y = jnp.dot(hid, w2_ref[0, 0].astype(_BF16), preferred_element_type=_F32)
            y = y * jnp.tile(w_row, (1, y.shape[1] // w_row.shape[1]))
            o_ref[rows, :] = _pack_halves(y.astype(_BF16).astype(_F32))

        @pl.when(jnp.logical_and(i * EXPERT_PAIR + k >= nused_ref[0], i == (nused_ref[0] - 1) // EXPERT_PAIR))
        def _():
            o_ref[rows, :] = x_ref[rows, :o_ref.shape[1]]


def _experts(layer, blk_e, nused, xg, w1, w3, w2):
    R, Dx = xg.shape
    bm = EXPERT_BLOCK * EXPERT_PAIR
    _, _, D, H = w1.shape
    rows = lambda i, blk, nu: (jnp.minimum(i, (nu[0] - 1) // EXPERT_PAIR), 0)
    w_specs = []
    for k in range(EXPERT_PAIR):
        wsel = functools.partial(lambda k, i, blk, nu: (layer, blk[i * EXPERT_PAIR + k], 0, 0), k)
        w_specs += [pl.BlockSpec((1, 1, D, H), wsel), pl.BlockSpec((1, 1, D, H), wsel), pl.BlockSpec((1, 1, H, D), wsel)]
    return pl.pallas_call(
        _expert_kernel,
        grid_spec=pltpu.PrefetchScalarGridSpec(
            num_scalar_prefetch=2,
            grid=(R // bm,),
            in_specs=[pl.BlockSpec((bm, Dx), rows)] + w_specs,
            out_specs=pl.BlockSpec((bm, D // 2), rows),
        ),
        out_shape=jax.ShapeDtypeStruct((R, Dx), jnp.uint32),
        input_output_aliases={2: 0},
        compiler_params=pltpu.CompilerParams(dimension_semantics=("arbitrary",), vmem_limit_bytes=VMEM_LIMIT),
        name="experts",
    )(blk_e, nused, xg, *([w1, w3, w2] * EXPERT_PAIR))


def _combine_kernel(alpha, cnt_ref, lstart_ref, goff_ref, tot_ref, x_ref, shift_ref, scale_ref, gate_ref,
                    p_ref, yg_ref, s1_ref, s3_ref, s2_ref, lg_ref, lb_ref, o_ref, ys_ref, sem):
    j = pl.program_id(0)
    cur = j % 2

    @pl.when(j == 0)
    def _():
        ys_ref[...] = jnp.zeros_like(ys_ref)
        _segment_copies(cnt_ref, lstart_ref, goff_ref, j, ys_ref.at[0], yg_ref, sem.at[0], False)

    last = pl.num_programs(0) - 1
    _segment_copies(cnt_ref, lstart_ref, goff_ref, jnp.minimum(j + 1, last), ys_ref.at[1 - cur], yg_ref,
                    sem.at[1 - cur], False, enable=j < last)

    x = x_ref[...]
    h2 = (x * (1.0 + scale_ref[0]) + shift_ref[0]).astype(_BF16)
    shared_hid = jnp.dot(h2, s1_ref[...], preferred_element_type=_F32)
    shared_hid = shared_hid * jax.nn.sigmoid(shared_hid) * jnp.dot(h2, s3_ref[...], preferred_element_type=_F32)
    shared = jnp.dot(shared_hid.astype(_BF16), s2_ref[...], preferred_element_type=_F32)

    _wait_rows(tot_ref, j, ys_ref.at[cur], yg_ref, sem.at[cur], False)
    onehot = p_ref[0]
    lo, hi = _unpack_halves(ys_ref[cur])
    sum_rows = (((0,), (0,)), ((), ()))
    routed = jnp.concatenate([lax.dot_general(onehot, lo, sum_rows, preferred_element_type=_F32),
                              lax.dot_general(onehot, hi, sum_rows, preferred_element_type=_F32)], axis=1)
    o_ref[...] = _layer_norm(alpha * x + gate_ref[0] * (routed + shared), lg_ref[...], lb_ref[...])


def _combine(alpha, tables, x1, shift, scale, gate, onehot, yg, s1, s3, s2, lg, lb, tiles_per_seq):
    T, D = x1.shape
    tm = TOKEN_TILE
    nt = T // tm
    H = s1.shape[1]
    vec = pl.BlockSpec((1, 1, D), lambda j, *_: (j // tiles_per_seq, 0, 0))
    const = lambda shape: pl.BlockSpec(shape, lambda j, *_: (0,) * len(shape))
    return pl.pallas_call(
        functools.partial(_combine_kernel, alpha),
        grid_spec=pltpu.PrefetchScalarGridSpec(
            num_scalar_prefetch=4,
            grid=(nt,),
            in_specs=[pl.BlockSpec((tm, D), lambda j, *_: (j, 0)), vec, vec, vec,
                      pl.BlockSpec((1, SLOT_ROWS, tm), lambda j, *_: (j, 0, 0)),
                      pl.BlockSpec(memory_space=pl.ANY),
                      const((D, H)), const((D, H)), const((H, D)), const((1, D)), const((1, D))],
            out_specs=pl.BlockSpec((tm, D), lambda j, *_: (j, 0)),
            scratch_shapes=[pltpu.VMEM((2, SLOT_ROWS, D // 2), jnp.uint32), pltpu.SemaphoreType.DMA((2,))],
        ),
        out_shape=jax.ShapeDtypeStruct((T, D), _F32),
        compiler_params=pltpu.CompilerParams(dimension_semantics=("arbitrary",), vmem_limit_bytes=VMEM_LIMIT),
        name="combine",
    )(*tables, x1, shift, scale, gate, onehot, yg, s1, s3, s2, lg.reshape(1, D), lb.reshape(1, D))


def _pack_heads(w):
    H, d, _ = w.shape
    eye = jnp.eye(HEAD_PACK, dtype=w.dtype)
    packed = jnp.einsum("kiab,ij->kiajb", w.reshape(H // HEAD_PACK, HEAD_PACK, d, d), eye)
    return packed.reshape(H // HEAD_PACK, HEAD_PACK * d, HEAD_PACK * d)


def _round_up(v, m):
    return (v + m - 1) // m * m


def _dispatch_tables(cnt_f, n_blocks):
    cnt_true = cnt_f.astype(jnp.int32)
    cnt = _round_up(cnt_true, SUBLANES)
    lstart = jnp.cumsum(cnt, axis=1) - cnt
    seg = jnp.stack([lstart, lstart + cnt_true], axis=1).astype(_F32)
    seg = jnp.concatenate([seg, seg], axis=2)
    per_row = jnp.sum(cnt, axis=0)
    region = _round_up(per_row, EXPERT_BLOCK)
    region_end = jnp.cumsum(region)
    goff = (region_end - region)[None, :] + jnp.cumsum(cnt, axis=0) - cnt
    tot = jnp.sum(cnt, axis=1)
    nused = (region_end[-1] // EXPERT_BLOCK).reshape(1)
    blk_first = jnp.arange(n_blocks, dtype=jnp.int32) * EXPERT_BLOCK
    blk_row = jnp.sum((region_end[None, :] <= blk_first[:, None]).astype(jnp.int32), axis=1)
    blk_row = jnp.minimum(blk_row, N_EXPERTS - 1)
    blk_e = _expert_of_row(blk_row)
    tables = (cnt.reshape(-1), lstart.reshape(-1).astype(jnp.int32), goff.reshape(-1).astype(jnp.int32),
              tot.astype(jnp.int32))
    return tables, seg, blk_e.astype(jnp.int32), nused.astype(jnp.int32)


def kernel(x, c, ada_w, ada_b, w_in, sgu_ln_g, sgu_ln_b, sgu_w, sgu_b, conv_w, conv_b, rg_wa, rg_ba, rg_wx, rg_bx, rg_lambda, w_branch_a, w_branch_b, w_out, ln1_g, ln1_b, router_w, router_b, exp_w1, exp_w3, exp_w2, sh_w1, sh_w3, sh_w2, ln2_g, ln2_b):
    B, S, D = x.shape
    L = ada_w.shape[0]
    T = B * S
    tm = TOKEN_TILE
    assert S % (ROUTER_TILES * tm) == 0 and S % MIXER_TILE == 0 and MIXER_TILE % SGU_CHUNK == 0 and D % 256 == 0
    assert router_w.shape[2] == N_EXPERTS
    alpha = float((2 * L) ** 0.25)
    tiles_per_seq = S // tm
    nt = T // tm
    n_blocks = _round_up((T * TOP_K + nt * N_EXPERTS * (SUBLANES - 1)) // EXPERT_BLOCK + N_EXPERTS + 1, EXPERT_PAIR)
    total_rows = n_blocks * EXPERT_BLOCK
    row_perm = _expert_of_row(jnp.arange(N_EXPERTS))

    xg = jnp.zeros((total_rows, D // 2 + ROW_WORDS_EXTRA), jnp.uint32)
    ada = _ada_all(c, ada_w, ada_b)
    bf = lambda a: a.astype(_BF16)
    for l in range(L):
        shift1, scale1, gate1, shift2, scale2, gate2 = [
            ada[l, :, k * D:(k + 1) * D].reshape(B, 1, D) for k in range(6)]
        x1 = _mixer(alpha, x, shift1, scale1, gate1, bf(w_in[l]), sgu_ln_g[l], sgu_ln_b[l], sgu_w[l],
                    sgu_b[l].T, conv_w[l], conv_b[l], bf(_pack_heads(rg_wa[l])), rg_ba[l], bf(_pack_heads(rg_wx[l])), rg_bx[l],
                    rg_lambda[l], bf(w_branch_a[l]), bf(w_branch_b[l]), bf(w_out[l]), ln1_g[l], ln1_b[l])
        x1 = x1.reshape(T, D)
        wr_rows = router_w[l].T[row_perm]
        br_rows = router_b[l][row_perm].reshape(N_EXPERTS, 1)
        digits, wn, cnt = _router(x1, shift2, scale2, wr_rows, br_rows, tiles_per_seq)
        tables, seg, blk_e, nused = _dispatch_tables(cnt[:, 0, :], n_blocks)
        wn_t = jnp.pad(wn.transpose(0, 2, 1), ((0, 0), (0, 0), (0, N_EXPERTS)))
        xg, onehot = _dispatch(tables, x1, shift2, scale2, digits, wn_t, seg, xg, tiles_per_seq)
        xg = _experts(l, blk_e, nused, xg, exp_w1, exp_w3, exp_w2)
        x2 = _combine(alpha, tables, x1, shift2, scale2, gate2, onehot,
                      xg, bf(sh_w1[l]), bf(sh_w3[l]), bf(sh_w2[l]), ln2_g[l], ln2_b[l], tiles_per_seq)
        x = x2.reshape(B, S, D)
    return x
```

```python
import functools

import jax
import jax.numpy as jnp
from jax import lax
from jax.experimental import pallas as pl
from jax.experimental.pallas import tpu as pltpu

SGU_CHUNK = 128
SGU_GROUPS = 8
RG_HEADS = 8
CONV_WIDTH = 4
RG_C = 8.0
N_EXPERTS = 64
TOP_K = 8
N_GROUPS = 8
GROUP_SIZE = N_EXPERTS // N_GROUPS
TOPK_GROUPS = 4
ROUTED_SCALE = 2.5
LN_EPS = 1e-5

SUBLANES = 8
TOKEN_TILE = 256
MIXER_TILE = 512
ROUTER_TILES = 8
SLOT_ROWS = TOP_K * TOKEN_TILE + N_EXPERTS * SUBLANES
EXPERT_BLOCK = 1024
EXPERT_PAIR = 2
SLOT_CHUNK = 2304
SLOT_TAIL = 256
SLOT_RADIX = 64.0
SLOT_NONE = 255.0
OWNER_LANES = 2 * N_EXPERTS
ROW_WORDS_EXTRA = 128
HEAD_PACK = 2
VMEM_LIMIT = 56 * 1024 * 1024

_F32 = jnp.float32
_BF16 = jnp.bfloat16


_GELU_K1 = 2.0 * (2.0 / 3.141592653589793) ** 0.5
_GELU_K2 = _GELU_K1 * 0.044715


def _sigmoid(x):
    return 0.5 * jnp.tanh(0.5 * x) + 0.5


def _gelu(x):
    half = 0.5 * x
    return half + half * jnp.tanh(x * (0.5 * _GELU_K1 + 0.5 * _GELU_K2 * (x * x)))


def _layer_norm(x, g, b):
    mu = jnp.mean(x, axis=-1, keepdims=True)
    xc = x - mu
    var = jnp.mean(xc * xc, axis=-1, keepdims=True)
    return xc * lax.rsqrt(var + LN_EPS) * g + b


def _pack_halves(v):
    n = v.shape[1] // 2
    lo = lax.bitcast_convert_type(v[:, :n], jnp.uint32)
    hi = lax.bitcast_convert_type(v[:, n:], jnp.uint32)
    return (hi & jnp.uint32(0xFFFF0000)) | (lo >> 16)


def _unpack_halves(u):
    lo = lax.bitcast_convert_type(u << 16, _F32).astype(_BF16)
    hi = lax.bitcast_convert_type(u & jnp.uint32(0xFFFF0000), _F32).astype(_BF16)
    return lo, hi


def _ada_kernel(c_ref, w_ref, b_ref, o_ref):
    c = c_ref[...]
    c_act = c * jax.nn.sigmoid(c)
    o_ref[0] = jnp.dot(c_act, w_ref[0], preferred_element_type=_F32,
                       precision=lax.Precision.HIGHEST) + b_ref[0]


def _ada_all(c, ada_w, ada_b):
    L, D, W = ada_w.shape
    B = c.shape[0]
    nb = W // D
    return pl.pallas_call(
        _ada_kernel,
        grid=(L, nb),
        in_specs=[pl.BlockSpec((B, D), lambda l, n: (0, 0)),
                  pl.BlockSpec((1, D, D), lambda l, n: (l, 0, n)),
                  pl.BlockSpec((1, 1, D), lambda l, n: (l, 0, n))],
        out_specs=pl.BlockSpec((1, B, D), lambda l, n: (l, 0, n)),
        out_shape=jax.ShapeDtypeStruct((L, B, W), _F32),
        name="ada",
    )(c, ada_w, ada_b.reshape(L, 1, W))


def _scan_rows(a, b, carry):
    tm, C = a.shape
    sub = lax.broadcasted_iota(jnp.int32, (SUBLANES, C), 0)
    steps = [(d, sub >= d) for d in (1, 2, 4)]
    tiles = []
    for k in range(tm // SUBLANES):
        rows = slice(k * SUBLANES, (k + 1) * SUBLANES)
        ak, bk = a[rows], b[rows]
        for d, keep in steps:
            a_sh = jnp.where(keep, pltpu.roll(ak, d, axis=0), 1.0)
            b_sh = jnp.where(keep, pltpu.roll(bk, d, axis=0), 0.0)
            bk = bk + ak * b_sh
            ak = ak * a_sh
        h = bk + ak * carry
        carry = h[SUBLANES - 1:SUBLANES]
        tiles.append(h)
    return jnp.concatenate(tiles, axis=0)


def _mixer_kernel(alpha, x_ref, shift_ref, scale_ref, gate_ref, win_ref, slg_ref, slb_ref, sw_ref, sbt_ref,
                  cw_ref, cb_ref, wa_ref, ba_ref, wx_ref, bx_ref, lam_ref, wba_ref, wbb_ref, wo_ref,
                  lg_ref, lb_ref, o_ref, rbuf, hprev, sp_ref):
    tm, D = x_ref.shape[1], x_ref.shape[2]
    C = SGU_CHUNK
    si = pl.program_id(1)

    @pl.when(si == 0)
    def _():
        rbuf[0:SUBLANES, :] = jnp.zeros((SUBLANES, D), _F32)
        hprev[...] = jnp.zeros_like(hprev)

    x = x_ref[0]
    h = (x * (1.0 + scale_ref[0]) + shift_ref[0]).astype(_BF16)

    def proj(k):
        return jnp.dot(h, win_ref[:, k * D:(k + 1) * D], preferred_element_type=_F32)

    vn = _layer_norm(_gelu(proj(1)), slg_ref[...], slb_ref[...]).astype(_BF16)
    tri = (lax.broadcasted_iota(jnp.int32, (C, C), 0) >= lax.broadcasted_iota(jnp.int32, (C, C), 1))
    for g in range(SGU_GROUPS):
        wm = jnp.where(tri, sw_ref[g], 0.0).astype(_BF16)
        bias = sbt_ref[:, g:g + 1]
        for cc in range(tm // C):
            blk = jnp.dot(wm, vn[cc * C:(cc + 1) * C, g * C:(g + 1) * C], preferred_element_type=_F32)
            sp_ref[cc * C:(cc + 1) * C, g * C:(g + 1) * C] = blk + bias
    y_a = (_gelu(proj(0)) * sp_ref[...]).astype(_BF16)

    rbuf[SUBLANES:SUBLANES + tm, :] = proj(3)
    r_in = cb_ref[...] + cw_ref[CONV_WIDTH - 1:CONV_WIDTH, :] * rbuf[SUBLANES:SUBLANES + tm, :]
    for dlt in range(1, CONV_WIDTH):
        r_in = r_in + cw_ref[CONV_WIDTH - 1 - dlt:CONV_WIDTH - dlt, :] * rbuf[SUBLANES - dlt:SUBLANES - dlt + tm, :]
    rbuf[0:SUBLANES, :] = rbuf[tm:tm + SUBLANES, :]
    r_bf = r_in.astype(_BF16)
    hd = wa_ref.shape[1]
    ra = jnp.concatenate([jnp.dot(r_bf[:, k * hd:(k + 1) * hd], wa_ref[k], preferred_element_type=_F32)
                          for k in range(D // hd)], axis=1)
    rx = jnp.concatenate([jnp.dot(r_bf[:, k * hd:(k + 1) * hd], wx_ref[k], preferred_element_type=_F32)
                          for k in range(D // hd)], axis=1)
    r_gate = _sigmoid(ra + ba_ref[...])
    i_gate = _sigmoid(rx + bx_ref[...])
    neg_lam = -lam_ref[...]
    softplus = jnp.maximum(neg_lam, 0.0) + jnp.log1p(jnp.exp(-jnp.abs(neg_lam)))
    log_a = (-RG_C) * r_gate * softplus
    a = jnp.exp(log_a)
    t = jnp.tanh(log_a)
    inp = jnp.sqrt(-2.0 * t / (1.0 - t)) * (i_gate * r_in)
    h_seq = _scan_rows(a, inp, hprev[...])
    hprev[...] = h_seq[tm - 1:tm, :]
    y_b = (_gelu(proj(2)) * h_seq).astype(_BF16)

    merged = (_sigmoid(proj(4)) * jnp.dot(y_a, wba_ref[...], preferred_element_type=_F32)
              + _sigmoid(proj(5)) * jnp.dot(y_b, wbb_ref[...], preferred_element_type=_F32))
    mix = jnp.dot(merged.astype(_BF16), wo_ref[...], preferred_element_type=_F32)
    o_ref[0] = _layer_norm(alpha * x + gate_ref[0] * mix, lg_ref[...], lb_ref[...])


def _const_spec(shape):
    nd = len(shape)
    return pl.BlockSpec(shape, lambda b, s: (0,) * nd, pipeline_mode=pl.Buffered(1))


def _mixer(alpha, x, shift, scale, gate, w_in, slg, slb, sw, sbt, cw, cb, wa, ba, wx, bx, lam, wba, wbb, wo, lg, lb):
    B, S, D = x.shape
    tm = MIXER_TILE
    vec = pl.BlockSpec((1, 1, D), lambda b, s: (b, 0, 0))
    row = lambda a: a.reshape(1, D)
    consts = [w_in, row(slg), row(slb), sw, sbt, cw, row(cb), wa, row(ba), wx, row(bx), row(lam), wba, wbb, wo,
              row(lg), row(lb)]
    return pl.pallas_call(
        functools.partial(_mixer_kernel, alpha),
        grid=(B, S // tm),
        in_specs=[pl.BlockSpec((1, tm, D), lambda b, s: (b, s, 0)), vec, vec, vec]
                 + [_const_spec(a.shape) for a in consts],
        out_specs=pl.BlockSpec((1, tm, D), lambda b, s: (b, s, 0)),
        out_shape=jax.ShapeDtypeStruct((B, S, D), _F32),
        scratch_shapes=[pltpu.VMEM((tm + SUBLANES, D), _F32), pltpu.VMEM((1, D), _F32), pltpu.VMEM((tm, D), _F32)],
        compiler_params=pltpu.CompilerParams(dimension_semantics=("arbitrary", "arbitrary"),
                                             vmem_limit_bytes=VMEM_LIMIT),
        name="mixer",
    )(x, shift, scale, gate, *consts)


def _expert_of_row(p):
    return (p % N_GROUPS) * GROUP_SIZE + p // N_GROUPS


def _router_kernel(x_ref, shift_ref, scale_ref, wr_ref, br_ref, digits_ref, wn_ref, cnt_ref):
    tm = TOKEN_TILE
    G, J = N_GROUPS, GROUP_SIZE
    h2 = x_ref[...] * (1.0 + scale_ref[0]) + shift_ref[0]
    wr = wr_ref[...]
    w_hi, h_hi = wr.astype(_BF16), h2.astype(_BF16)
    w_lo, h_lo = (wr - w_hi.astype(_F32)).astype(_BF16), (h2 - h_hi.astype(_F32)).astype(_BF16)
    nt_dims = (((1,), (1,)), ((), ()))
    logits = (lax.dot_general(w_hi, h_hi, nt_dims, preferred_element_type=_F32)
              + lax.dot_general(w_hi, h_lo, nt_dims, preferred_element_type=_F32)
              + lax.dot_general(w_lo, h_hi, nt_dims, preferred_element_type=_F32))
    scores = jax.nn.sigmoid(logits)
    biased = scores + br_ref[...]
    slabs = [biased[j * G:(j + 1) * G, :] for j in range(J)]

    m1, m2 = slabs[0], jnp.full_like(slabs[0], -jnp.inf)
    for v in slabs[1:]:
        m2 = jnp.maximum(m2, jnp.minimum(m1, v))
        m1 = jnp.maximum(m1, v)
    gs = m1 + m2
    gidx = lax.broadcasted_iota(jnp.int32, gs.shape, 0)
    grank = jnp.zeros(gs.shape, jnp.int32)
    for g2 in range(G):
        other = gs[g2:g2 + 1, :]
        beats = (other > gs) | ((other == gs) & (g2 < gidx))
        grank = grank + beats.astype(jnp.int32)
    keep = grank < TOPK_GROUPS
    masked = [jnp.where(keep, v, -jnp.inf) for v in slabs]

    eidx = [gidx * J + j for j in range(J)]
    picked = [jnp.zeros(gs.shape, jnp.bool_) for _ in range(J)]
    for _ in range(TOP_K):
        m = masked[0]
        for j in range(1, J):
            m = jnp.maximum(m, masked[j])
        top = jnp.max(m, axis=0, keepdims=True)
        cand = jnp.where(masked[0] == top, eidx[0], N_EXPERTS)
        for j in range(1, J):
            cand = jnp.minimum(cand, jnp.where(masked[j] == top, eidx[j], N_EXPERTS))
        first = jnp.min(cand, axis=0, keepdims=True)
        for j in range(J):
            hit = eidx[j] == first
            picked[j] = picked[j] | hit
            masked[j] = jnp.where(hit, -jnp.inf, masked[j])
    sel_all = jnp.concatenate(picked, axis=0)
    w_sel = scores * sel_all.astype(_F32)
    wn_all = w_sel / jnp.sum(w_sel, axis=0, keepdims=True) * ROUTED_SCALE

    E = N_EXPERTS
    upper = (lax.broadcasted_iota(jnp.int32, (tm, tm), 0) < lax.broadcasted_iota(jnp.int32, (tm, tm), 1))
    lower = (lax.broadcasted_iota(jnp.int32, (E, E), 1) < lax.broadcasted_iota(jnp.int32, (E, E), 0))
    ones = jnp.ones((SUBLANES, tm), _BF16)
    for t in range(ROUTER_TILES):
        sel = sel_all[:, t * tm:(t + 1) * tm]
        sel_f = sel.astype(_F32)
        sel_b = sel_f.astype(_BF16)
        wn_ref[t] = wn_all[:, t * tm:(t + 1) * tm]
        tok_rank = jnp.dot(sel_b, upper.astype(_BF16), preferred_element_type=_F32)
        cnt = jnp.sum(sel_f, axis=1, keepdims=True)
        cnt_tiles = jnp.floor((cnt + (SUBLANES - 1)) * (1.0 / SUBLANES))
        start = jnp.dot(lower.astype(_BF16), jnp.broadcast_to(cnt_tiles, (E, 128)).astype(_BF16),
                        preferred_element_type=_F32)[:, 0:1] * float(SUBLANES)
        slot1 = start + tok_rank + 1.0
        hi = jnp.floor(slot1 * (1.0 / SLOT_RADIX))
        lo = slot1 - hi * SLOT_RADIX
        digits = jnp.concatenate([jnp.where(sel, hi, SLOT_NONE), jnp.where(sel, lo, 0.0)], axis=1)
        digits_ref[t] = jnp.concatenate([digits, jnp.zeros_like(digits)], axis=0).astype(_BF16)
        cnt_ref[t] = lax.dot_general(ones, sel_b, (((1,), (1,)), ((), ())), preferred_element_type=_F32)


def _router(x1, shift, scale, wr_rows, br_rows, tiles_per_seq):
    T, D = x1.shape
    tm = TOKEN_TILE
    nt = T // tm
    rt = ROUTER_TILES
    vec = pl.BlockSpec((1, 1, D), lambda j: (j * rt // tiles_per_seq, 0, 0))
    return pl.pallas_call(
        _router_kernel,
        grid=(nt // rt,),
        in_specs=[pl.BlockSpec((rt * tm, D), lambda j: (j, 0)), vec, vec,
                  pl.BlockSpec((N_EXPERTS, D), lambda j: (0, 0)),
                  pl.BlockSpec((N_EXPERTS, 1), lambda j: (0, 0))],
        out_specs=[pl.BlockSpec((rt, OWNER_LANES, 2 * tm), lambda j: (j, 0, 0)),
                   pl.BlockSpec((rt, N_EXPERTS, tm), lambda j: (j, 0, 0)),
                   pl.BlockSpec((rt, SUBLANES, N_EXPERTS), lambda j: (j, 0, 0))],
        out_shape=[jax.ShapeDtypeStruct((nt, OWNER_LANES, 2 * tm), _BF16),
                   jax.ShapeDtypeStruct((nt, N_EXPERTS, tm), _F32),
                   jax.ShapeDtypeStruct((nt, SUBLANES, N_EXPERTS), _F32)],
        compiler_params=pltpu.CompilerParams(dimension_semantics=("arbitrary",), vmem_limit_bytes=VMEM_LIMIT),
        name="router",
    )(x1, shift, scale, wr_rows, br_rows)


def _segment_copies(cnt_ref, lstart_ref, goff_ref, j, local_ref, global_ref, sem, to_global, enable=None):
    for p in range(N_EXPERTS):
        n = pl.multiple_of(cnt_ref[j * N_EXPERTS + p], SUBLANES)
        ls = pl.multiple_of(lstart_ref[j * N_EXPERTS + p], SUBLANES)
        go = pl.multiple_of(goff_ref[j * N_EXPERTS + p], SUBLANES)

        @pl.when((n > 0) if enable is None else jnp.logical_and(n > 0, enable))
        def _():
            loc = local_ref.at[pl.ds(ls, n)]
            glo = global_ref.at[pl.ds(go, n), pl.ds(0, local_ref.shape[1])]
            if to_global:
                pltpu.make_async_copy(loc, glo, sem).start()
            else:
                pltpu.make_async_copy(glo, loc, sem).start()


def _wait_rows(tot_ref, j, local_ref, global_ref, sem, to_global):
    n = pl.multiple_of(tot_ref[j], SUBLANES)

    @pl.when(n > 0)
    def _():
        loc = local_ref.at[pl.ds(0, n)]
        glo = global_ref.at[pl.ds(0, n), pl.ds(0, local_ref.shape[1])]
        if to_global:
            pltpu.make_async_copy(loc, glo, sem).wait()
        else:
            pltpu.make_async_copy(glo, loc, sem).wait()


def _dispatch_kernel(cnt_ref, lstart_ref, goff_ref, tot_ref, x_ref, shift_ref, scale_ref, digits_ref, wn_ref, seg_ref,
                     xg_prev_ref, xg_ref, p_ref, xs_ref, sem):
    del xg_prev_ref
    j = pl.program_id(0)
    tm, D = x_ref.shape
    E = N_EXPERTS
    cur = j % 2
    prev = jnp.maximum(j - 1, 0)
    _segment_copies(cnt_ref, lstart_ref, goff_ref, prev, xs_ref.at[1 - cur], xg_ref, sem.at[1 - cur], True,
                    enable=j > 0)
    h2 = (x_ref[...] * (1.0 + scale_ref[0]) + shift_ref[0]).astype(_BF16)
    wn = wn_ref[0]
    w1 = wn.astype(_BF16)
    r1 = wn - w1.astype(_F32)
    w2 = r1.astype(_BF16)
    w3 = (r1 - w2.astype(_F32)).astype(_BF16)
    w12 = (w1.astype(_F32) + pltpu.roll(w2.astype(_F32), E, axis=1)).astype(_BF16)
    src = jnp.concatenate([h2, w12, w3], axis=1)
    digits = digits_ref[0]
    seg_lo, seg_hi = seg_ref[0, 0:1, :], seg_ref[0, 1:2, :]
    def sort_rows(first, size):
        rows = slice(first, first + size)
        rows_e = lax.broadcasted_iota(jnp.int32, (size, OWNER_LANES), 0).astype(_F32) + float(first)
        owner = jnp.where((rows_e >= seg_lo) & (rows_e < seg_hi), 1.0, 0.0)
        dig = jnp.dot(owner.astype(_BF16), digits, preferred_element_type=_F32)
        rows_t = lax.broadcasted_iota(jnp.int32, (size, tm), 0).astype(_F32) + float(first + 1)
        onehot = jnp.where(dig[:, :tm] * SLOT_RADIX + dig[:, tm:] == rows_t, 1.0, 0.0).astype(_BF16)
        p_ref[0, rows, :] = onehot
        picked = jnp.dot(onehot, src, preferred_element_type=_F32)
        w_row = jnp.sum(owner * (picked[:, D:D + OWNER_LANES] + picked[:, D + OWNER_LANES:]), axis=1, keepdims=True)
        xs_ref[cur, rows, :D // 2] = _pack_halves(picked[:, :D])
        xs_ref[cur, rows, D // 2:] = lax.bitcast_convert_type(
            jnp.broadcast_to(w_row, (size, ROW_WORDS_EXTRA)), jnp.uint32)

    body_rows = SLOT_ROWS - SLOT_TAIL
    for first in range(0, body_rows, SLOT_CHUNK):
        sort_rows(first, min(SLOT_CHUNK, body_rows - first))
    tail_used = tot_ref[j] > body_rows

    @pl.when(tail_used)
    def _():
        sort_rows(body_rows, SLOT_TAIL)

    @pl.when(jnp.logical_not(tail_used))
    def _():
        p_ref[0, body_rows:, :] = jnp.zeros((SLOT_TAIL, tm), _BF16)

    @pl.when(j > 0)
    def _():
        _wait_rows(tot_ref, j - 1, xs_ref.at[1 - cur], xg_ref, sem.at[1 - cur], True)

    @pl.when(j == pl.num_programs(0) - 1)
    def _():
        _segment_copies(cnt_ref, lstart_ref, goff_ref, j, xs_ref.at[cur], xg_ref, sem.at[cur], True)
        _wait_rows(tot_ref, j, xs_ref.at[cur], xg_ref, sem.at[cur], True)


def _dispatch(tables, x1, shift, scale, digits, wn_t, seg_rows, xg_prev, tiles_per_seq):
    T, D = x1.shape
    tm = TOKEN_TILE
    nt = T // tm
    vec = pl.BlockSpec((1, 1, D), lambda j, *_: (j // tiles_per_seq, 0, 0))
    return pl.pallas_call(
        _dispatch_kernel,
        grid_spec=pltpu.PrefetchScalarGridSpec(
            num_scalar_prefetch=4,
            grid=(nt,),
            in_specs=[pl.BlockSpec((tm, D), lambda j, *_: (j, 0)), vec, vec,
                      pl.BlockSpec((1, OWNER_LANES, 2 * tm), lambda j, *_: (j, 0, 0)),
                      pl.BlockSpec((1, tm, OWNER_LANES), lambda j, *_: (j, 0, 0)),
                      pl.BlockSpec((1, 2, OWNER_LANES), lambda j, *_: (j, 0, 0)),
                      pl.BlockSpec(memory_space=pl.ANY)],
            out_specs=[pl.BlockSpec(memory_space=pl.ANY),
                       pl.BlockSpec((1, SLOT_ROWS, tm), lambda j, *_: (j, 0, 0))],
            scratch_shapes=[pltpu.VMEM((2, SLOT_ROWS, D // 2 + ROW_WORDS_EXTRA), jnp.uint32),
                            pltpu.SemaphoreType.DMA((2,))],
        ),
        out_shape=[jax.ShapeDtypeStruct(xg_prev.shape, jnp.uint32),
                   jax.ShapeDtypeStruct((nt, SLOT_ROWS, tm), _BF16)],
        input_output_aliases={10: 0},
        compiler_params=pltpu.CompilerParams(dimension_semantics=("arbitrary",), vmem_limit_bytes=VMEM_LIMIT),
        name="dispatch",
    )(*tables, x1, shift, scale, digits, wn_t, seg_rows, xg_prev)


def _expert_kernel(blk_ref, nused_ref, x_ref, *refs):
    o_ref = refs[-1]
    i = pl.program_id(0)
    bm = EXPERT_BLOCK
    for k in range(EXPERT_PAIR):
        w1_ref, w3_ref, w2_ref = refs[3 * k:3 * k + 3]
        rows = slice(k * bm, (k + 1) * bm)

        @pl.when(i * EXPERT_PAIR + k < nused_ref[0])
        def _():
            half = w1_ref.shape[2] // 2
            lo, hi = _unpack_halves(x_ref[rows, :half])
            w_row = lax.bitcast_convert_type(x_ref[rows, half:], _F32)
            w1 = w1_ref[0, 0].astype(_BF16)
            w3 = w3_ref[0, 0].astype(_BF16)
            a = (jnp.dot(lo, w1[:half], preferred_element_type=_F32)
                 + jnp.dot(hi, w1[half:], preferred_element_type=_F32))
            b = (jnp.dot(lo, w3[:half], preferred_element_type=_F32)
                 + jnp.dot(hi, w3[half:], preferred_element_type=_F32))
            hid = (a * jax.nn.sigmoid(a) * b).astype(_BF16)
            y = jnp.dot(hid, w2_ref[0, 0].astype(_BF16), preferred_element_type=_F32)
            y = y * jnp.tile(w_row, (1, y.shape[1] // w_row.shape[1]))
            o_ref[rows, :] = _pack_halves(y.astype(_BF16).astype(_F32))

        @pl.when(jnp.logical_and(i * EXPERT_PAIR + k >= nused_ref[0], i == (nused_ref[0] - 1) // EXPERT_PAIR))
        def _():
            o_ref[rows, :] = x_ref[rows, :o_ref.shape[1]]


def _experts(layer, blk_e, nused, xg, w1, w3, w2):
    R, Dx = xg.shape
    bm = EXPERT_BLOCK * EXPERT_PAIR
    _, _, D, H = w1.shape
    rows = lambda i, blk, nu: (jnp.minimum(i, (nu[0] - 1) // EXPERT_PAIR), 0)
    w_specs = []
    for k in range(EXPERT_PAIR):
        wsel = functools.partial(lambda k, i, blk, nu: (layer, blk[i * EXPERT_PAIR + k], 0, 0), k)
        w_specs += [pl.BlockSpec((1, 1, D, H), wsel), pl.BlockSpec((1, 1, D, H), wsel), pl.BlockSpec((1, 1, H, D), wsel)]
    return pl.pallas_call(
        _expert_kernel,
        grid_spec=pltpu.PrefetchScalarGridSpec(
            num_scalar_prefetch=2,
            grid=(R // bm,),
            in_specs=[pl.BlockSpec((bm, Dx), rows)] + w_specs,
            out_specs=pl.BlockSpec((bm, D // 2), rows),
        ),
        out_shape=jax.ShapeDtypeStruct((R, Dx), jnp.uint32),
        input_output_aliases={2: 0},
        compiler_params=pltpu.CompilerParams(dimension_semantics=("arbitrary",), vmem_limit_bytes=VMEM_LIMIT),
        name="experts",
    )(blk_e, nused, xg, *([w1, w3, w2] * EXPERT_PAIR))


def _combine_kernel(alpha, cnt_ref, lstart_ref, goff_ref, tot_ref, x_ref, shift_ref, scale_ref, gate_ref,
                    p_ref, yg_ref, s1_ref, s3_ref, s2_ref, lg_ref, lb_ref, o_ref, ys_ref, sem):
    j = pl.program_id(0)
    cur = j % 2

    @pl.when(j == 0)
    def _():
        ys_ref[...] = jnp.zeros_like(ys_ref)
        _segment_copies(cnt_ref, lstart_ref, goff_ref, j, ys_ref.at[0], yg_ref, sem.at[0], False)

    last = pl.num_programs(0) - 1
    _segment_copies(cnt_ref, lstart_ref, goff_ref, jnp.minimum(j + 1, last), ys_ref.at[1 - cur], yg_ref,
                    sem.at[1 - cur], False, enable=j < last)

    x = x_ref[...]
    h2 = (x * (1.0 + scale_ref[0]) + shift_ref[0]).astype(_BF16)
    shared_hid = jnp.dot(h2, s1_ref[...], preferred_element_type=_F32)
    shared_hid = shared_hid * jax.nn.sigmoid(shared_hid) * jnp.dot(h2, s3_ref[...], preferred_element_type=_F32)
    shared = jnp.dot(shared_hid.astype(_BF16), s2_ref[...], preferred_element_type=_F32)

    _wait_rows(tot_ref, j, ys_ref.at[cur], yg_ref, sem.at[cur], False)
    onehot = p_ref[0]
    lo, hi = _unpack_halves(ys_ref[cur])
    sum_rows = (((0,), (0,)), ((), ()))
    routed = jnp.concatenate([lax.dot_general(onehot, lo, sum_rows, preferred_element_type=_F32),
                              lax.dot_general(onehot, hi, sum_rows, preferred_element_type=_F32)], axis=1)
    o_ref[...] = _layer_norm(alpha * x + gate_ref[0] * (routed + shared), lg_ref[...], lb_ref[...])


def _combine(alpha, tables, x1, shift, scale, gate, onehot, yg, s1, s3, s2, lg, lb, tiles_per_seq):
    T, D = x1.shape
    tm = TOKEN_TILE
    nt = T // tm
    H = s1.shape[1]
    vec = pl.BlockSpec((1, 1, D), lambda j, *_: (j // tiles_per_seq, 0, 0))
    const = lambda shape: pl.BlockSpec(shape, lambda j, *_: (0,) * len(shape))
    return pl.pallas_call(
        functools.partial(_combine_kernel, alpha),
        grid_spec=pltpu.PrefetchScalarGridSpec(
            num_scalar_prefetch=4,
            grid=(nt,),
            in_specs=[pl.BlockSpec((tm, D), lambda j, *_: (j, 0)), vec, vec, vec,
                      pl.BlockSpec((1, SLOT_ROWS, tm), lambda j, *_: (j, 0, 0)),
                      pl.BlockSpec(memory_space=pl.ANY),
                      const((D, H)), const((D, H)), const((H, D)), const((1, D)), const((1, D))],
            out_specs=pl.BlockSpec((tm, D), lambda j, *_: (j, 0)),
            scratch_shapes=[pltpu.VMEM((2, SLOT_ROWS, D // 2), jnp.uint32), pltpu.SemaphoreType.DMA((2,))],
        ),
        out_shape=jax.ShapeDtypeStruct((T, D), _F32),
        compiler_params=pltpu.CompilerParams(dimension_semantics=("arbitrary",), vmem_limit_bytes=VMEM_LIMIT),
        name="combine",
    )(*tables, x1, shift, scale, gate, onehot, yg, s1, s3, s2, lg.reshape(1, D), lb.reshape(1, D))


def _pack_heads(w):
    H, d, _ = w.shape
    eye = jnp.eye(HEAD_PACK, dtype=w.dtype)
    packed = jnp.einsum("kiab,ij->kiajb", w.reshape(H // HEAD_PACK, HEAD_PACK, d, d), eye)
    return packed.reshape(H // HEAD_PACK, HEAD_PACK * d, HEAD_PACK * d)


def _round_up(v, m):
    return (v + m - 1) // m * m


def _dispatch_tables(cnt_f, n_blocks):
    cnt_true = cnt_f.astype(jnp.int32)
    cnt = _round_up(cnt_true, SUBLANES)
    lstart = jnp.cumsum(cnt, axis=1) - cnt
    seg = jnp.stack([lstart, lstart + cnt_true], axis=1).astype(_F32)
    seg = jnp.concatenate([seg, seg], axis=2)
    per_row = jnp.sum(cnt, axis=0)
    region = _round_up(per_row, EXPERT_BLOCK)
    region_end = jnp.cumsum(region)
    goff = (region_end - region)[None, :] + jnp.cumsum(cnt, axis=0) - cnt
    tot = jnp.sum(cnt, axis=1)
    nused = (region_end[-1] // EXPERT_BLOCK).reshape(1)
    blk_first = jnp.arange(n_blocks, dtype=jnp.int32) * EXPERT_BLOCK
    blk_row = jnp.sum((region_end[None, :] <= blk_first[:, None]).astype(jnp.int32), axis=1)
    blk_row = jnp.minimum(blk_row, N_EXPERTS - 1)
    blk_e = _expert_of_row(blk_row)
    tables = (cnt.reshape(-1), lstart.reshape(-1).astype(jnp.int32), goff.reshape(-1).astype(jnp.int32),
              tot.astype(jnp.int32))
    return tables, seg, blk_e.astype(jnp.int32), nused.astype(jnp.int32)


def kernel(x, c, ada_w, ada_b, w_in, sgu_ln_g, sgu_ln_b, sgu_w, sgu_b, conv_w, conv_b, rg_wa, rg_ba, rg_wx, rg_bx, rg_lambda, w_branch_a, w_branch_b, w_out, ln1_g, ln1_b, router_w, router_b, exp_w1, exp_w3, exp_w2, sh_w1, sh_w3, sh_w2, ln2_g, ln2_b):
    B, S, D = x.shape
    L = ada_w.shape[0]
    T = B * S
    tm = TOKEN_TILE
    assert S % (ROUTER_TILES * tm) == 0 and S % MIXER_TILE == 0 and MIXER_TILE % SGU_CHUNK == 0 and D % 256 == 0
    assert router_w.shape[2] == N_EXPERTS
    alpha = float((2 * L) ** 0.25)
    tiles_per_seq = S // tm
    nt = T // tm
    n_blocks = _round_up((T * TOP_K + nt * N_EXPERTS * (SUBLANES - 1)) // EXPERT_BLOCK + N_EXPERTS + 1, EXPERT_PAIR)
    total_rows = n_blocks * EXPERT_BLOCK
    row_perm = _expert_of_row(jnp.arange(N_EXPERTS))

    xg = jnp.zeros((total_rows, D // 2 + ROW_WORDS_EXTRA), jnp.uint32)
    ada = _ada_all(c, ada_w, ada_b)
    bf = lambda a: a.astype(_BF16)
    for l in range(L):
        shift1, scale1, gate1, shift2, scale2, gate2 = [
            ada[l, :, k * D:(k + 1) * D].reshape(B, 1, D) for k in range(6)]
        x1 = _mixer(alpha, x, shift1, scale1, gate1, bf(w_in[l]), sgu_ln_g[l], sgu_ln_b[l], sgu_w[l],
                    sgu_b[l].T, conv_w[l], conv_b[l], bf(_pack_heads(rg_wa[l])), rg_ba[l], bf(_pack_heads(rg_wx[l])), rg_bx[l],
                    rg_lambda[l], bf(w_branch_a[l]), bf(w_branch_b[l]), bf(w_out[l]), ln1_g[l], ln1_b[l])
        x1 = x1.reshape(T, D)
        wr_rows = router_w[l].T[row_perm]
        br_rows = router_b[l][row_perm].reshape(N_EXPERTS, 1)
        digits, wn, cnt = _router(x1, shift2, scale2, wr_rows, br_rows, tiles_per_seq)
        tables, seg, blk_e, nused = _dispatch_tables(cnt[:, 0, :], n_blocks)
        wn_t = jnp.pad(wn.transpose(0, 2, 1), ((0, 0), (0, 0), (0, N_EXPERTS)))
        xg, onehot = _dispatch(tables, x1, shift2, scale2, digits, wn_t, seg, xg, tiles_per_seq)
        xg = _experts(l, blk_e, nused, xg, exp_w1, exp_w3, exp_w2)
        x2 = _combine(alpha, tables, x1, shift2, scale2, gate2, onehot,
                      xg, bf(sh_w1[l]), bf(sh_w3[l]), bf(sh_w2[l]), ln2_g[l], ln2_b[l], tiles_per_seq)
        x = x2.reshape(B, S, D)
    return x
```

```python
import functools

import jax
import jax.numpy as jnp
from jax import lax
from jax.experimental import pallas as pl
from jax.experimental.pallas import tpu as pltpu

SGU_CHUNK = 128
SGU_GROUPS = 8
RG_HEADS = 8
CONV_WIDTH = 4
RG_C = 8.0
N_EXPERTS = 64
TOP_K = 8
N_GROUPS = 8
GROUP_SIZE = N_EXPERTS // N_GROUPS
TOPK_GROUPS = 4
ROUTED_SCALE = 2.5
LN_EPS = 1e-5

SUBLANES = 8
TOKEN_TILE = 256
MIXER_TILE = 512
ROUTER_TILES = 8
SLOT_ROWS = TOP_K * TOKEN_TILE + N_EXPERTS * SUBLANES
EXPERT_BLOCK = 1024
EXPERT_PAIR = 2
SLOT_CHUNK = 1280
SLOT_TAIL = 256
SLOT_RADIX = 64.0
SLOT_NONE = 255.0
OWNER_LANES = 2 * N_EXPERTS
ROW_WORDS_EXTRA = 128
HEAD_PACK = 2
VMEM_LIMIT = 56 * 1024 * 1024

_F32 = jnp.float32
_BF16 = jnp.bfloat16


_GELU_K1 = 2.0 * (2.0 / 3.141592653589793) ** 0.5
_GELU_K2 = _GELU_K1 * 0.044715


def _sigmoid(x):
    return 0.5 * jnp.tanh(0.5 * x) + 0.5


def _gelu(x):
    half = 0.5 * x
    return half + half * jnp.tanh(x * (0.5 * _GELU_K1 + 0.5 * _GELU_K2 * (x * x)))


def _layer_norm(x, g, b):
    mu = jnp.mean(x, axis=-1, keepdims=True)
    xc = x - mu
    var = jnp.mean(xc * xc, axis=-1, keepdims=True)
    return xc * lax.rsqrt(var + LN_EPS) * g + b


def _pack_halves(v):
    n = v.shape[1] // 2
    lo = lax.bitcast_convert_type(v[:, :n], jnp.uint32)
    hi = lax.bitcast_convert_type(v[:, n:], jnp.uint32)
    return (hi & jnp.uint32(0xFFFF0000)) | (lo >> 16)


def _unpack_halves(u):
    lo = lax.bitcast_convert_type(u << 16, _F32).astype(_BF16)
    hi = lax.bitcast_convert_type(u & jnp.uint32(0xFFFF0000), _F32).astype(_BF16)
    return lo, hi


def _ada_kernel(c_ref, w_ref, b_ref, o_ref):
    c = c_ref[...]
    c_act = c * jax.nn.sigmoid(c)
    o_ref[0] = jnp.dot(c_act, w_ref[0], preferred_element_type=_F32,
                       precision=lax.Precision.HIGHEST) + b_ref[0]


def _ada_all(c, ada_w, ada_b):
    L, D, W = ada_w.shape
    B = c.shape[0]
    nb = W // D
    return pl.pallas_call(
        _ada_kernel,
        grid=(L, nb),
        in_specs=[pl.BlockSpec((B, D), lambda l, n: (0, 0)),
                  pl.BlockSpec((1, D, D), lambda l, n: (l, 0, n)),
                  pl.BlockSpec((1, 1, D), lambda l, n: (l, 0, n))],
        out_specs=pl.BlockSpec((1, B, D), lambda l, n: (l, 0, n)),
        out_shape=jax.ShapeDtypeStruct((L, B, W), _F32),
        name="ada",
    )(c, ada_w, ada_b.reshape(L, 1, W))


def _scan_rows(a, b, carry):
    tm, C = a.shape
    sub = lax.broadcasted_iota(jnp.int32, (SUBLANES, C), 0)
    steps = [(d, sub >= d) for d in (1, 2, 4)]
    tiles = []
    for k in range(tm // SUBLANES):
        rows = slice(k * SUBLANES, (k + 1) * SUBLANES)
        ak, bk = a[rows], b[rows]
        for d, keep in steps:
            a_sh = jnp.where(keep, pltpu.roll(ak, d, axis=0), 1.0)
            b_sh = jnp.where(keep, pltpu.roll(bk, d, axis=0), 0.0)
            bk = bk + ak * b_sh
            ak = ak * a_sh
        h = bk + ak * carry
        carry = h[SUBLANES - 1:SUBLANES]
        tiles.append(h)
    return jnp.concatenate(tiles, axis=0)


def _mixer_kernel(alpha, x_ref, shift_ref, scale_ref, gate_ref, win_ref, slg_ref, slb_ref, sw_ref, sbt_ref,
                  cw_ref, cb_ref, wa_ref, ba_ref, wx_ref, bx_ref, lam_ref, wba_ref, wbb_ref, wo_ref,
                  lg_ref, lb_ref, o_ref, rbuf, hprev, sp_ref):
    tm, D = x_ref.shape[1], x_ref.shape[2]
    C = SGU_CHUNK
    si = pl.program_id(1)

    @pl.when(si == 0)
    def _():
        rbuf[0:SUBLANES, :] = jnp.zeros((SUBLANES, D), _F32)
        hprev[...] = jnp.zeros_like(hprev)

    x = x_ref[0]
    h = (x * (1.0 + scale_ref[0]) + shift_ref[0]).astype(_BF16)

    def proj(k):
        return jnp.dot(h, win_ref[:, k * D:(k + 1) * D], preferred_element_type=_F32)

    vn = _layer_norm(_gelu(proj(1)), slg_ref[...], slb_ref[...]).astype(_BF16)
    tri = (lax.broadcasted_iota(jnp.int32, (C, C), 0) >= lax.broadcasted_iota(jnp.int32, (C, C), 1))
    for g in range(SGU_GROUPS):
        wm = jnp.where(tri, sw_ref[g], 0.0).astype(_BF16)
        bias = sbt_ref[:, g:g + 1]
        for cc in range(tm // C):
            blk = jnp.dot(wm, vn[cc * C:(cc + 1) * C, g * C:(g + 1) * C], preferred_element_type=_F32)
            sp_ref[cc * C:(cc + 1) * C, g * C:(g + 1) * C] = blk + bias
    y_a = (_gelu(proj(0)) * sp_ref[...]).astype(_BF16)

    rbuf[SUBLANES:SUBLANES + tm, :] = proj(3)
    r_in = cb_ref[...] + cw_ref[CONV_WIDTH - 1:CONV_WIDTH, :] * rbuf[SUBLANES:SUBLANES + tm, :]
    for dlt in range(1, CONV_WIDTH):
        r_in = r_in + cw_ref[CONV_WIDTH - 1 - dlt:CONV_WIDTH - dlt, :] * rbuf[SUBLANES - dlt:SUBLANES - dlt + tm, :]
    rbuf[0:SUBLANES, :] = rbuf[tm:tm + SUBLANES, :]
    r_bf = r_in.astype(_BF16)
    hd = wa_ref.shape[1]
    ra = jnp.concatenate([jnp.dot(r_bf[:, k * hd:(k + 1) * hd], wa_ref[k], preferred_element_type=_F32)
                          for k in range(D // hd)], axis=1)
    rx = jnp.concatenate([jnp.dot(r_bf[:, k * hd:(k + 1) * hd], wx_ref[k], preferred_element_type=_F32)
                          for k in range(D // hd)], axis=1)
    r_gate = _sigmoid(ra + ba_ref[...])
    i_gate = _sigmoid(rx + bx_ref[...])
    neg_lam = -lam_ref[...]
    softplus = jnp.maximum(neg_lam, 0.0) + jnp.log1p(jnp.exp(-jnp.abs(neg_lam)))
    log_a = (-RG_C) * r_gate * softplus
    a = jnp.exp(log_a)
    t = jnp.tanh(log_a)
    inp = jnp.sqrt(-2.0 * t / (1.0 - t)) * (i_gate * r_in)
    h_seq = _scan_rows(a, inp, hprev[...])
    hprev[...] = h_seq[tm - 1:tm, :]
    y_b = (_gelu(proj(2)) * h_seq).astype(_BF16)

    merged = (_sigmoid(proj(4)) * jnp.dot(y_a, wba_ref[...], preferred_element_type=_F32)
              + _sigmoid(proj(5)) * jnp.dot(y_b, wbb_ref[...], preferred_element_type=_F32))
    mix = jnp.dot(merged.astype(_BF16), wo_ref[...], preferred_element_type=_F32)
    o_ref[0] = _layer_norm(alpha * x + gate_ref[0] * mix, lg_ref[...], lb_ref[...])


def _const_spec(shape):
    nd = len(shape)
    return pl.BlockSpec(shape, lambda b, s: (0,) * nd, pipeline_mode=pl.Buffered(1))


def _mixer(alpha, x, shift, scale, gate, w_in, slg, slb, sw, sbt, cw, cb, wa, ba, wx, bx, lam, wba, wbb, wo, lg, lb):
    B, S, D = x.shape
    tm = MIXER_TILE
    vec = pl.BlockSpec((1, 1, D), lambda b, s: (b, 0, 0))
    row = lambda a: a.reshape(1, D)
    consts = [w_in, row(slg), row(slb), sw, sbt, cw, row(cb), wa, row(ba), wx, row(bx), row(lam), wba, wbb, wo,
              row(lg), row(lb)]
    return pl.pallas_call(
        functools.partial(_mixer_kernel, alpha),
        grid=(B, S // tm),
        in_specs=[pl.BlockSpec((1, tm, D), lambda b, s: (b, s, 0)), vec, vec, vec]
                 + [_const_spec(a.shape) for a in consts],
        out_specs=pl.BlockSpec((1, tm, D), lambda b, s: (b, s, 0)),
        out_shape=jax.ShapeDtypeStruct((B, S, D), _F32),
        scratch_shapes=[pltpu.VMEM((tm + SUBLANES, D), _F32), pltpu.VMEM((1, D), _F32), pltpu.VMEM((tm, D), _F32)],
        compiler_params=pltpu.CompilerParams(dimension_semantics=("arbitrary", "arbitrary"),
                                             vmem_limit_bytes=VMEM_LIMIT),
        name="mixer",
    )(x, shift, scale, gate, *consts)


def _expert_of_row(p):
    return (p % N_GROUPS) * GROUP_SIZE + p // N_GROUPS


def _router_kernel(x_ref, shift_ref, scale_ref, wr_ref, br_ref, digits_ref, wn_ref, cnt_ref):
    tm = TOKEN_TILE
    G, J = N_GROUPS, GROUP_SIZE
    h2 = x_ref[...] * (1.0 + scale_ref[0]) + shift_ref[0]
    wr = wr_ref[...]
    w_hi, h_hi = wr.astype(_BF16), h2.astype(_BF16)
    w_lo, h_lo = (wr - w_hi.astype(_F32)).astype(_BF16), (h2 - h_hi.astype(_F32)).astype(_BF16)
    nt_dims = (((1,), (1,)), ((), ()))
    logits = (lax.dot_general(w_hi, h_hi, nt_dims, preferred_element_type=_F32)
              + lax.dot_general(w_hi, h_lo, nt_dims, preferred_element_type=_F32)
              + lax.dot_general(w_lo, h_hi, nt_dims, preferred_element_type=_F32))
    scores = jax.nn.sigmoid(logits)
    biased = scores + br_ref[...]
    slabs = [biased[j * G:(j + 1) * G, :] for j in range(J)]

    m1, m2 = slabs[0], jnp.full_like(slabs[0], -jnp.inf)
    for v in slabs[1:]:
        m2 = jnp.maximum(m2, jnp.minimum(m1, v))
        m1 = jnp.maximum(m1, v)
    gs = m1 + m2
    gidx = lax.broadcasted_iota(jnp.int32, gs.shape, 0)
    grank = jnp.zeros(gs.shape, jnp.int32)
    for g2 in range(G):
        other = gs[g2:g2 + 1, :]
        beats = (other > gs) | ((other == gs) & (g2 < gidx))
        grank = grank + beats.astype(jnp.int32)
    keep = grank < TOPK_GROUPS
    masked = [jnp.where(keep, v, -jnp.inf) for v in slabs]

    eidx = [gidx * J + j for j in range(J)]
    picked = [jnp.zeros(gs.shape, jnp.bool_) for _ in range(J)]
    for _ in range(TOP_K):
        m = masked[0]
        for j in range(1, J):
            m = jnp.maximum(m, masked[j])
        top = jnp.max(m, axis=0, keepdims=True)
        cand = jnp.where(masked[0] == top, eidx[0], N_EXPERTS)
        for j in range(1, J):
            cand = jnp.minimum(cand, jnp.where(masked[j] == top, eidx[j], N_EXPERTS))
        first = jnp.min(cand, axis=0, keepdims=True)
        for j in range(J):
            hit = eidx[j] == first
            picked[j] = picked[j] | hit
            masked[j] = jnp.where(hit, -jnp.inf, masked[j])
    sel_all = jnp.concatenate(picked, axis=0)
    w_sel = scores * sel_all.astype(_F32)
    wn_all = w_sel / jnp.sum(w_sel, axis=0, keepdims=True) * ROUTED_SCALE

    E = N_EXPERTS
    upper = (lax.broadcasted_iota(jnp.int32, (tm, tm), 0) < lax.broadcasted_iota(jnp.int32, (tm, tm), 1))
    lower = (lax.broadcasted_iota(jnp.int32, (E, E), 1) < lax.broadcasted_iota(jnp.int32, (E, E), 0))
    ones = jnp.ones((SUBLANES, tm), _BF16)
    for t in range(ROUTER_TILES):
        sel = sel_all[:, t * tm:(t + 1) * tm]
        sel_f = sel.astype(_F32)
        sel_b = sel_f.astype(_BF16)
        wn_ref[t] = wn_all[:, t * tm:(t + 1) * tm]
        tok_rank = jnp.dot(sel_b, upper.astype(_BF16), preferred_element_type=_F32)
        cnt = jnp.sum(sel_f, axis=1, keepdims=True)
        cnt_tiles = jnp.floor((cnt + (SUBLANES - 1)) * (1.0 / SUBLANES))
        start = jnp.dot(lower.astype(_BF16), jnp.broadcast_to(cnt_tiles, (E, 128)).astype(_BF16),
                        preferred_element_type=_F32)[:, 0:1] * float(SUBLANES)
        slot1 = start + tok_rank + 1.0
        hi = jnp.floor(slot1 * (1.0 / SLOT_RADIX))
        lo = slot1 - hi * SLOT_RADIX
        digits = jnp.concatenate([jnp.where(sel, hi, SLOT_NONE), jnp.where(sel, lo, 0.0)], axis=1)
        digits_ref[t] = jnp.concatenate([digits, jnp.zeros_like(digits)], axis=0).astype(_BF16)
        cnt_ref[t] = lax.dot_general(ones, sel_b, (((1,), (1,)), ((), ())), preferred_element_type=_F32)


def _router(x1, shift, scale, wr_rows, br_rows, tiles_per_seq):
    T, D = x1.shape
    tm = TOKEN_TILE
    nt = T // tm
    rt = ROUTER_TILES
    vec = pl.BlockSpec((1, 1, D), lambda j: (j * rt // tiles_per_seq, 0, 0))
    return pl.pallas_call(
        _router_kernel,
        grid=(nt // rt,),
        in_specs=[pl.BlockSpec((rt * tm, D), lambda j: (j, 0)), vec, vec,
                  pl.BlockSpec((N_EXPERTS, D), lambda j: (0, 0)),
                  pl.BlockSpec((N_EXPERTS, 1), lambda j: (0, 0))],
        out_specs=[pl.BlockSpec((rt, OWNER_LANES, 2 * tm), lambda j: (j, 0, 0)),
                   pl.BlockSpec((rt, N_EXPERTS, tm), lambda j: (j, 0, 0)),
                   pl.BlockSpec((rt, SUBLANES, N_EXPERTS), lambda j: (j, 0, 0))],
        out_shape=[jax.ShapeDtypeStruct((nt, OWNER_LANES, 2 * tm), _BF16),
                   jax.ShapeDtypeStruct((nt, N_EXPERTS, tm), _F32),
                   jax.ShapeDtypeStruct((nt, SUBLANES, N_EXPERTS), _F32)],
        compiler_params=pltpu.CompilerParams(dimension_semantics=("arbitrary",), vmem_limit_bytes=VMEM_LIMIT),
        name="router",
    )(x1, shift, scale, wr_rows, br_rows)


def _segment_copies(cnt_ref, lstart_ref, goff_ref, j, local_ref, global_ref, sem, to_global, enable=None):
    for p in range(N_EXPERTS):
        n = pl.multiple_of(cnt_ref[j * N_EXPERTS + p], SUBLANES)
        ls = pl.multiple_of(lstart_ref[j * N_EXPERTS + p], SUBLANES)
        go = pl.multiple_of(goff_ref[j * N_EXPERTS + p], SUBLANES)

        @pl.when((n > 0) if enable is None else jnp.logical_and(n > 0, enable))
        def _():
            loc = local_ref.at[pl.ds(ls, n)]
            glo = global_ref.at[pl.ds(go, n), pl.ds(0, local_ref.shape[1])]
            if to_global:
                pltpu.make_async_copy(loc, glo, sem).start()
            else:
                pltpu.make_async_copy(glo, loc, sem).start()


def _wait_rows(tot_ref, j, local_ref, global_ref, sem, to_global):
    n = pl.multiple_of(tot_ref[j], SUBLANES)

    @pl.when(n > 0)
    def _():
        loc = local_ref.at[pl.ds(0, n)]
        glo = global_ref.at[pl.ds(0, n), pl.ds(0, local_ref.shape[1])]
        if to_global:
            pltpu.make_async_copy(loc, glo, sem).wait()
        else:
            pltpu.make_async_copy(glo, loc, sem).wait()


def _dispatch_kernel(cnt_ref, lstart_ref, goff_ref, tot_ref, x_ref, shift_ref, scale_ref, digits_ref, wn_ref, seg_ref,
                     xg_prev_ref, xg_ref, p_ref, xs_ref, sem):
    del xg_prev_ref
    j = pl.program_id(0)
    tm, D = x_ref.shape
    E = N_EXPERTS
    cur = j % 2
    h2 = (x_ref[...] * (1.0 + scale_ref[0]) + shift_ref[0]).astype(_BF16)
    wn = wn_ref[0]
    w1 = wn.astype(_BF16)
    r1 = wn - w1.astype(_F32)
    w2 = r1.astype(_BF16)
    w3 = (r1 - w2.astype(_F32)).astype(_BF16)
    w12 = (w1.astype(_F32) + pltpu.roll(w2.astype(_F32), E, axis=1)).astype(_BF16)
    src = jnp.concatenate([h2, w12, w3], axis=1)
    digits = digits_ref[0]
    seg_lo, seg_hi = seg_ref[0, 0:1, :], seg_ref[0, 1:2, :]
    def sort_rows(first, size):
        rows = slice(first, first + size)
        rows_e = lax.broadcasted_iota(jnp.int32, (size, OWNER_LANES), 0).astype(_F32) + float(first)
        owner = jnp.where((rows_e >= seg_lo) & (rows_e < seg_hi), 1.0, 0.0)
        dig = jnp.dot(owner.astype(_BF16), digits, preferred_element_type=_F32)
        rows_t = lax.broadcasted_iota(jnp.int32, (size, tm), 0).astype(_F32) + float(first + 1)
        onehot = jnp.where(dig[:, :tm] * SLOT_RADIX + dig[:, tm:] == rows_t, 1.0, 0.0).astype(_BF16)
        p_ref[0, rows, :] = onehot
        picked = jnp.dot(onehot, src, preferred_element_type=_F32)
        w_row = jnp.sum(owner * (picked[:, D:D + OWNER_LANES] + picked[:, D + OWNER_LANES:]), axis=1, keepdims=True)
        xs_ref[cur, rows, :D // 2] = _pack_halves(picked[:, :D])
        xs_ref[cur, rows, D // 2:] = lax.bitcast_convert_type(
            jnp.broadcast_to(w_row, (size, ROW_WORDS_EXTRA)), jnp.uint32)

    body_rows = SLOT_ROWS - SLOT_TAIL
    for first in range(0, body_rows, SLOT_CHUNK):
        sort_rows(first, min(SLOT_CHUNK, body_rows - first))
    tail_used = tot_ref[j] > body_rows

    @pl.when(tail_used)
    def _():
        sort_rows(body_rows, SLOT_TAIL)

    @pl.when(jnp.logical_not(tail_used))
    def _():
        p_ref[0, body_rows:, :] = jnp.zeros((SLOT_TAIL, tm), _BF16)

    _segment_copies(cnt_ref, lstart_ref, goff_ref, j, xs_ref.at[cur], xg_ref, sem.at[cur], True)

    @pl.when(j > 0)
    def _():
        _wait_rows(tot_ref, j - 1, xs_ref.at[1 - cur], xg_ref, sem.at[1 - cur], True)

    @pl.when(j == pl.num_programs(0) - 1)
    def _():
        _wait_rows(tot_ref, j, xs_ref.at[cur], xg_ref, sem.at[cur], True)


def _dispatch(tables, x1, shift, scale, digits, wn_t, seg_rows, xg_prev, tiles_per_seq):
    T, D = x1.shape
    tm = TOKEN_TILE
    nt = T // tm
    vec = pl.BlockSpec((1, 1, D), lambda j, *_: (j // tiles_per_seq, 0, 0))
    return pl.pallas_call(
        _dispatch_kernel,
        grid_spec=pltpu.PrefetchScalarGridSpec(
            num_scalar_prefetch=4,
            grid=(nt,),
            in_specs=[pl.BlockSpec((tm, D), lambda j, *_: (j, 0)), vec, vec,
                      pl.BlockSpec((1, OWNER_LANES, 2 * tm), lambda j, *_: (j, 0, 0)),
                      pl.BlockSpec((1, tm, OWNER_LANES), lambda j, *_: (j, 0, 0)),
                      pl.BlockSpec((1, 2, OWNER_LANES), lambda j, *_: (j, 0, 0)),
                      pl.BlockSpec(memory_space=pl.ANY)],
            out_specs=[pl.BlockSpec(memory_space=pl.ANY),
                       pl.BlockSpec((1, SLOT_ROWS, tm), lambda j, *_: (j, 0, 0))],
            scratch_shapes=[pltpu.VMEM((2, SLOT_ROWS, D // 2 + ROW_WORDS_EXTRA), jnp.uint32),
                            pltpu.SemaphoreType.DMA((2,))],
        ),
        out_shape=[jax.ShapeDtypeStruct(xg_prev.shape, jnp.uint32),
                   jax.ShapeDtypeStruct((nt, SLOT_ROWS, tm), _BF16)],
        input_output_aliases={10: 0},
        compiler_params=pltpu.CompilerParams(dimension_semantics=("arbitrary",), vmem_limit_bytes=VMEM_LIMIT),
        name="dispatch",
    )(*tables, x1, shift, scale, digits, wn_t, seg_rows, xg_prev)


def _expert_kernel(blk_ref, nused_ref, x_ref, *refs):
    o_ref = refs[-1]
    i = pl.program_id(0)
    bm = EXPERT_BLOCK
    for k in range(EXPERT_PAIR):
        w1_ref, w3_ref, w2_ref = refs[3 * k:3 * k + 3]
        rows = slice(k * bm, (k + 1) * bm)

        @pl.when(i * EXPERT_PAIR + k < nused_ref[0])
        def _():
            half = w1_ref.shape[2] // 2
            lo, hi = _unpack_halves(x_ref[rows, :half])
            w_row = lax.bitcast_convert_type(x_ref[rows, half:], _F32)
            w1 = w1_ref[0, 0].astype(_BF16)
            w3 = w3_ref[0, 0].astype(_BF16)
            a = (jnp.dot(lo, w1[:half], preferred_element_type=_F32)
                 + jnp.dot(hi, w1[half:], preferred_element_type=_F32))
            b = (jnp.dot(lo, w3[:half], preferred_element_type=_F32)
                 + jnp.dot(hi, w3[half:], preferred_element_type=_F32))
            hid = (a * jax.nn.sigmoid(a) * b).astype(_BF16)
            y = jnp.dot(hid, w2_ref[0, 0].astype(_BF16), preferred_element_type=_F32)
            y = y * jnp.tile(w_row, (1, y.shape[1] // w_row.shape[1]))
            o_ref[rows, :] = _pack_halves(y.astype(_BF16).astype(_F32))

        @pl.when(jnp.logical_and(i * EXPERT_PAIR + k >= nused_ref[0], i == (nused_ref[0] - 1) // EXPERT_PAIR))
        def _():
            o_ref[rows, :] = x_ref[rows, :o_ref.shape[1]]


def _experts(layer, blk_e, nused, xg, w1, w3, w2):
    R, Dx = xg.shape
    bm = EXPERT_BLOCK * EXPERT_PAIR
    _, _, D, H = w1.shape
    rows = lambda i, blk, nu: (jnp.minimum(i, (nu[0] - 1) // EXPERT_PAIR), 0)
    w_specs = []
    for k in range(EXPERT_PAIR):
        wsel = functools.partial(lambda k, i, blk, nu: (layer, blk[i * EXPERT_PAIR + k], 0, 0), k)
        w_specs += [pl.BlockSpec((1, 1, D, H), wsel), pl.BlockSpec((1, 1, D, H), wsel), pl.BlockSpec((1, 1, H, D), wsel)]
    return pl.pallas_call(
        _expert_kernel,
        grid_spec=pltpu.PrefetchScalarGridSpec(
            num_scalar_prefetch=2,
            grid=(R // bm,),
            in_specs=[pl.BlockSpec((bm, Dx), rows)] + w_specs,
            out_specs=pl.BlockSpec((bm, D // 2), rows),
        ),
        out_shape=jax.ShapeDtypeStruct((R, Dx), jnp.uint32),
        input_output_aliases={2: 0},
        compiler_params=pltpu.CompilerParams(dimension_semantics=("arbitrary",), vmem_limit_bytes=VMEM_LIMIT),
        name="experts",
    )(blk_e, nused, xg, *([w1, w3, w2] * EXPERT_PAIR))


def _combine_kernel(alpha, cnt_ref, lstart_ref, goff_ref, tot_ref, x_ref, shift_ref, scale_ref, gate_ref,
                    p_ref, yg_ref, s1_ref, s3_ref, s2_ref, lg_ref, lb_ref, o_ref, ys_ref, sem):
    j = pl.program_id(0)
    cur = j % 2

    @pl.when(j == 0)
    def _():
        ys_ref[...] = jnp.zeros_like(ys_ref)
        _segment_copies(cnt_ref, lstart_ref, goff_ref, j, ys_ref.at[0], yg_ref, sem.at[0], False)

    last = pl.num_programs(0) - 1
    _segment_copies(cnt_ref, lstart_ref, goff_ref, jnp.minimum(j + 1, last), ys_ref.at[1 - cur], yg_ref,
                    sem.at[1 - cur], False, enable=j < last)

    x = x_ref[...]
    h2 = (x * (1.0 + scale_ref[0]) + shift_ref[0]).astype(_BF16)
    shared_hid = jnp.dot(h2, s1_ref[...], preferred_element_type=_F32)
    shared_hid = shared_hid * jax.nn.sigmoid(shared_hid) * jnp.dot(h2, s3_ref[...], preferred_element_type=_F32)
    shared = jnp.dot(shared_hid.astype(_BF16), s2_ref[...], preferred_element_type=_F32)

    _wait_rows(tot_ref, j, ys_ref.at[cur], yg_ref, sem.at[cur], False)
    onehot = p_ref[0]
    lo, hi = _unpack_halves(ys_ref[cur])
    sum_rows = (((0,), (0,)), ((), ()))
    routed = jnp.concatenate([lax.dot_general(onehot, lo, sum_rows, preferred_element_type=_F32),
                              lax.dot_general(onehot, hi, sum_rows, preferred_element_type=_F32)], axis=1)
    o_ref[...] = _layer_norm(alpha * x + gate_ref[0] * (routed + shared), lg_ref[...], lb_ref[...])


def _combine(alpha, tables, x1, shift, scale, gate, onehot, yg, s1, s3, s2, lg, lb, tiles_per_seq):
    T, D = x1.shape
    tm = TOKEN_TILE
    nt = T // tm
    H = s1.shape[1]
    vec = pl.BlockSpec((1, 1, D), lambda j, *_: (j // tiles_per_seq, 0, 0))
    const = lambda shape: pl.BlockSpec(shape, lambda j, *_: (0,) * len(shape))
    return pl.pallas_call(
        functools.partial(_combine_kernel, alpha),
        grid_spec=pltpu.PrefetchScalarGridSpec(
            num_scalar_prefetch=4,
            grid=(nt,),
            in_specs=[pl.BlockSpec((tm, D), lambda j, *_: (j, 0)), vec, vec, vec,
                      pl.BlockSpec((1, SLOT_ROWS, tm), lambda j, *_: (j, 0, 0)),
                      pl.BlockSpec(memory_space=pl.ANY),
                      const((D, H)), const((D, H)), const((H, D)), const((1, D)), const((1, D))],
            out_specs=pl.BlockSpec((tm, D), lambda j, *_: (j, 0)),
            scratch_shapes=[pltpu.VMEM((2, SLOT_ROWS, D // 2), jnp.uint32), pltpu.SemaphoreType.DMA((2,))],
        ),
        out_shape=jax.ShapeDtypeStruct((T, D), _F32),
        compiler_params=pltpu.CompilerParams(dimension_semantics=("arbitrary",), vmem_limit_bytes=VMEM_LIMIT),
        name="combine",
    )(*tables, x1, shift, scale, gate, onehot, yg, s1, s3, s2, lg.reshape(1, D), lb.reshape(1, D))


def _pack_heads(w):
    H, d, _ = w.shape
    eye = jnp.eye(HEAD_PACK, dtype=w.dtype)
    packed = jnp.einsum("kiab,ij->kiajb", w.reshape(H // HEAD_PACK, HEAD_PACK, d, d), eye)
    return packed.reshape(H // HEAD_PACK, HEAD_PACK * d, HEAD_PACK * d)


def _round_up(v, m):
    return (v + m - 1) // m * m


def _dispatch_tables(cnt_f, n_blocks):
    cnt_true = cnt_f.astype(jnp.int32)
    cnt = _round_up(cnt_true, SUBLANES)
    lstart = jnp.cumsum(cnt, axis=1) - cnt
    seg = jnp.stack([lstart, lstart + cnt_true], axis=1).astype(_F32)
    seg = jnp.concatenate([seg, seg], axis=2)
    per_row = jnp.sum(cnt, axis=0)
    region = _round_up(per_row, EXPERT_BLOCK)
    region_end = jnp.cumsum(region)
    goff = (region_end - region)[None, :] + jnp.cumsum(cnt, axis=0) - cnt
    tot = jnp.sum(cnt, axis=1)
    nused = (region_end[-1] // EXPERT_BLOCK).reshape(1)
    blk_first = jnp.arange(n_blocks, dtype=jnp.int32) * EXPERT_BLOCK
    blk_row = jnp.sum((region_end[None, :] <= blk_first[:, None]).astype(jnp.int32), axis=1)
    blk_row = jnp.minimum(blk_row, N_EXPERTS - 1)
    blk_e = _expert_of_row(blk_row)
    tables = (cnt.reshape(-1), lstart.reshape(-1).astype(jnp.int32), goff.reshape(-1).astype(jnp.int32),
              tot.astype(jnp.int32))
    return tables, seg, blk_e.astype(jnp.int32), nused.astype(jnp.int32)


def kernel(x, c, ada_w, ada_b, w_in, sgu_ln_g, sgu_ln_b, sgu_w, sgu_b, conv_w, conv_b, rg_wa, rg_ba, rg_wx, rg_bx, rg_lambda, w_branch_a, w_branch_b, w_out, ln1_g, ln1_b, router_w, router_b, exp_w1, exp_w3, exp_w2, sh_w1, sh_w3, sh_w2, ln2_g, ln2_b):
    B, S, D = x.shape
    L = ada_w.shape[0]
    T = B * S
    tm = TOKEN_TILE
    assert S % (ROUTER_TILES * tm) == 0 and S % MIXER_TILE == 0 and MIXER_TILE % SGU_CHUNK == 0 and D % 256 == 0
    assert router_w.shape[2] == N_EXPERTS
    alpha = float((2 * L) ** 0.25)
    tiles_per_seq = S // tm
    nt = T // tm
    n_blocks = _round_up((T * TOP_K + nt * N_EXPERTS * (SUBLANES - 1)) // EXPERT_BLOCK + N_EXPERTS + 1, EXPERT_PAIR)
    total_rows = n_blocks * EXPERT_BLOCK
    row_perm = _expert_of_row(jnp.arange(N_EXPERTS))

    xg = jnp.zeros((total_rows, D // 2 + ROW_WORDS_EXTRA), jnp.uint32)
    ada = _ada_all(c, ada_w, ada_b)
    bf = lambda a: a.astype(_BF16)
    for l in range(L):
        shift1, scale1, gate1, shift2, scale2, gate2 = [
            ada[l, :, k * D:(k + 1) * D].reshape(B, 1, D) for k in range(6)]
        x1 = _mixer(alpha, x, shift1, scale1, gate1, bf(w_in[l]), sgu_ln_g[l], sgu_ln_b[l], sgu_w[l],
                    sgu_b[l].T, conv_w[l], conv_b[l], bf(_pack_heads(rg_wa[l])), rg_ba[l], bf(_pack_heads(rg_wx[l])), rg_bx[l],
                    rg_lambda[l], bf(w_branch_a[l]), bf(w_branch_b[l]), bf(w_out[l]), ln1_g[l], ln1_b[l])
        x1 = x1.reshape(T, D)
        wr_rows = router_w[l].T[row_perm]
        br_rows = router_b[l][row_perm].reshape(N_EXPERTS, 1)
        digits, wn, cnt = _router(x1, shift2, scale2, wr_rows, br_rows, tiles_per_seq)
        tables, seg, blk_e, nused = _dispatch_tables(cnt[:, 0, :], n_blocks)
        wn_t = jnp.pad(wn.transpose(0, 2, 1), ((0, 0), (0, 0), (0, N_EXPERTS)))
        xg, onehot = _dispatch(tables, x1, shift2, scale2, digits, wn_t, seg, xg, tiles_per_seq)
        xg = _experts(l, blk_e, nused, xg, exp_w1, exp_w3, exp_w2)
        x2 = _combine(alpha, tables, x1, shift2, scale2, gate2, onehot,
                      xg, bf(sh_w1[l]), bf(sh_w3[l]), bf(sh_w2[l]), ln2_g[l], ln2_b[l], tiles_per_seq)
        x = x2.reshape(B, S, D)
    return x
```
